```python
import jax, jax.numpy as jnp
from jax import lax
import numpy as np

D_MODEL = 1024
BATCH = 4
SEQ = 4096
DEPTH = 4
DEC_BATCH = 32
DEC_SEQ = 4
PAST_LEN = 8192
PAGE_SIZE = 128

N_MIXERS = 2
N_LAYERS_A = (DEPTH + 1) // 2
N_LAYERS_B = DEPTH // 2
HEAD_DIM_A = 64
N_HEADS_A = D_MODEL // HEAD_DIM_A
D_A = N_HEADS_A * HEAD_DIM_A
MOBA_BLOCK = 256
MOBA_TOPK = 3
MOBA_Q_BLOCK = 64
HEAD_DIM_B = 128
N_GROUPS_B = 3
HEADS_PER_GROUP_B = 4
N_HEADS_B = N_GROUPS_B * HEADS_PER_GROUP_B
D_B_OUT = HEADS_PER_GROUP_B * HEAD_DIM_B
WINDOWS = (128, 512, 2048)
DILATIONS = (1, 4, 16)
Q_BLOCK = 128
D_FF = 2816
CONV_W = 3
RMS_EPS = 1e-6
NEG_INF = -1e30

kernel_name = "hybrid_moba_dilated_convffn_decode_step"


def _alibi_slopes(n_heads):
    return jnp.exp2(-8.0 * jnp.arange(1, n_heads + 1, dtype=jnp.float32) / n_heads)


def _rmsnorm(x, g):
    xf = x.astype(jnp.float32)
    y = xf * lax.rsqrt(jnp.mean(xf * xf, axis=-1, keepdims=True) + RMS_EPS)
    return (y * g.astype(jnp.float32)).astype(x.dtype)


def _modulate(x, shift, scale):
    return x * (1 + scale[:, None, :]) + shift[:, None, :]


def _ada(c, w, b):
    m = jax.nn.silu(c) @ w + b
    return jnp.split(m, 6, axis=-1)


def _moba_attend(q, q_pos, kb, vb, k_mean, slopes):
    B, Tq, H, dh = q.shape
    NB = kb.shape[1]
    n_sel = min(MOBA_TOPK, NB)
    own = q_pos // MOBA_BLOCK
    gate = jnp.einsum("bqhd,bnhd->bqhn", q, k_mean, preferred_element_type=jnp.float32)
    fully_past = jnp.arange(NB)[None, :] < own[:, None]
    gate = jnp.where(fully_past[None, :, None, :], gate, NEG_INF)
    _, sel = lax.top_k(gate, n_sel)
    blocks = jnp.concatenate(
        [sel, jnp.broadcast_to(own[None, :, None, None], (B, Tq, H, 1))], axis=-1)
    bi = jnp.arange(B)[:, None, None, None]
    hi = jnp.arange(H)[None, None, :, None]
    kg = kb[bi, blocks, :, hi]
    vg = vb[bi, blocks, :, hi]
    kpos = blocks[..., None] * MOBA_BLOCK + jnp.arange(MOBA_BLOCK)
    dist = q_pos[None, :, None, None, None] - kpos
    slot_ok = jnp.concatenate(
        [jnp.arange(n_sel)[None, :] < own[:, None], jnp.ones((Tq, 1), dtype=bool)], axis=-1)
    valid = slot_ok[None, :, None, :, None] & (dist >= 0)
    s = jnp.einsum("bqhd,bqhnkd->bqhnk", q, kg, preferred_element_type=jnp.float32) * (dh ** -0.5)
    s = s - slopes[None, None, :, None, None] * dist.astype(jnp.float32)
    s = jnp.where(valid, s, NEG_INF)
    p = jax.nn.softmax(s.reshape(B, Tq, H, -1), axis=-1).reshape(s.shape)
    o = jnp.einsum("bqhnk,bqhnkd->bqhd", p, vg.astype(jnp.float32))
    return o.astype(q.dtype)


def _moba_qkv(h, w_qkv):
    B, T, _ = h.shape
    qkv = (h @ w_qkv).reshape(B, T, 3, N_HEADS_A, HEAD_DIM_A)
    return qkv[:, :, 0], qkv[:, :, 1], qkv[:, :, 2]


def _to_blocks(x):
    B, T, H, dh = x.shape
    pad = (-T) % MOBA_BLOCK
    return jnp.pad(x, ((0, 0), (0, pad), (0, 0), (0, 0))).reshape(B, -1, MOBA_BLOCK, H, dh)


def _moba_prompt(h, w_qkv, w_o, slopes):
    B, T, _ = h.shape
    q, k, v = _moba_qkv(h, w_qkv)
    kb, vb = _to_blocks(k), _to_blocks(v)
    k_mean = jnp.mean(kb.astype(jnp.float32), axis=2)

    def blk(i):
        qi = lax.dynamic_slice_in_dim(q, i * MOBA_Q_BLOCK, MOBA_Q_BLOCK, axis=1)
        pos = i * MOBA_Q_BLOCK + jnp.arange(MOBA_Q_BLOCK)
        return _moba_attend(qi, pos, kb, vb, k_mean, slopes)

    o = lax.map(blk, jnp.arange(T // MOBA_Q_BLOCK))
    o = o.transpose(1, 0, 2, 3, 4).reshape(B, T, D_A)
    return o @ w_o, k, v


def _moba_sample(h, cache_k, cache_v, page_table, w_qkv, w_o, slopes):
    B, T, _ = h.shape
    q, k, v = _moba_qkv(h, w_qkv)
    k_past = cache_k[page_table].reshape(B, -1, N_HEADS_A, HEAD_DIM_A)
    v_past = cache_v[page_table].reshape(B, -1, N_HEADS_A, HEAD_DIM_A)
    past_len = k_past.shape[1]
    kb = _to_blocks(jnp.concatenate([k_past, k], axis=1))
    vb = _to_blocks(jnp.concatenate([v_past, v], axis=1))
    k_mean = jnp.mean(kb.astype(jnp.float32), axis=2)
    pos = past_len + jnp.arange(T)
    o = _moba_attend(q, pos, kb, vb, k_mean, slopes).reshape(B, T, D_A)
    return o @ w_o, k, v


def _dilated_group(q, k_src, v_src, q_pos, src_start, dilation, n_keys, slopes):
    dh = q.shape[-1]
    dist = dilation * jnp.arange(n_keys)
    kpos = q_pos[:, None] - dist[None, :]
    idx = jnp.clip(kpos - src_start, 0, k_src.shape[1] - 1)
    kg = k_src[:, idx]
    vg = v_src[:, idx]
    s = jnp.einsum("bqhd,bqnhd->bqhn", q, kg, preferred_element_type=jnp.float32) * (dh ** -0.5)
    s = s - slopes[None, None, :, None] * dist.astype(jnp.float32)[None, None, None, :]
    s = jnp.where((kpos >= 0)[None, :, None, :], s, NEG_INF)
    lse = jax.nn.logsumexp(s, axis=-1)
    p = jnp.exp(s - lse[..., None])
    o = jnp.einsum("bqhn,bqnhd->bqhd", p, vg.astype(jnp.float32))
    return o, lse


def _dilated_mix(q, k_srcs, v_srcs, q_pos, src_starts, slopes):
    B, Tq = q.shape[:2]
    outs, lses = [], []
    for g in range(N_GROUPS_B):
        o, lse = _dilated_group(q[:, :, g], k_srcs[g], v_srcs[g], q_pos, src_starts[g],
                                DILATIONS[g], WINDOWS[g] // DILATIONS[g] + 1, slopes[g])
        outs.append(o)
        lses.append(lse)
    w = jax.nn.softmax(jnp.stack(lses), axis=0)
    o = jnp.einsum("gbqh,gbqhd->bqhd", w, jnp.stack(outs))
    return o.reshape(B, Tq, D_B_OUT).astype(q.dtype)


def _dilated_qkv(h, w_qkv):
    B, T, _ = h.shape
    qkv = (h @ w_qkv).reshape(B, T, 3, N_GROUPS_B, HEADS_PER_GROUP_B, HEAD_DIM_B)
    return qkv[:, :, 0], qkv[:, :, 1:]


def _dilated_prompt(h, w_qkv, w_o, slopes):
    B, T, _ = h.shape
    q, kv = _dilated_qkv(h, w_qkv)
    k_srcs = [kv[:, :, 0, g] for g in range(N_GROUPS_B)]
    v_srcs = [kv[:, :, 1, g] for g in range(N_GROUPS_B)]
    starts = [0] * N_GROUPS_B

    def blk(i):
        qi = lax.dynamic_slice_in_dim(q, i * Q_BLOCK, Q_BLOCK, axis=1)
        pos = i * Q_BLOCK + jnp.arange(Q_BLOCK)
        return _dilated_mix(qi, k_srcs, v_srcs, pos, starts, slopes)

    o = lax.map(blk, jnp.arange(T // Q_BLOCK))
    o = o.transpose(1, 0, 2, 3).reshape(B, T, D_B_OUT)
    new = [kv[:, -min(WINDOWS[g], T):, :, g] for g in range(N_GROUPS_B)]
    return o @ w_o, new


def _dilated_sample(h, bufs, w_qkv, w_o, slopes):
    B, T, _ = h.shape
    q, kv = _dilated_qkv(h, w_qkv)
    k_srcs, v_srcs, starts, new = [], [], [], []
    for g in range(N_GROUPS_B):
        wb = bufs[g].shape[1]
        cat = jnp.concatenate([bufs[g], kv[:, :, :, g]], axis=1)
        k_srcs.append(cat[:, :, 0])
        v_srcs.append(cat[:, :, 1])
        starts.append(PAST_LEN - wb)
        new.append(cat[:, -wb:])
    pos = PAST_LEN + jnp.arange(T)
    o = _dilated_mix(q, k_srcs, v_srcs, pos, starts, slopes)
    return o @ w_o, new


def _conv_ffn(h, buf, w_in, conv_w, conv_b, w_down):
    T = h.shape[1]
    a, b = jnp.split(h @ w_in, 2, axis=-1)
    a_ext = jnp.concatenate([buf, a], axis=1)
    y = conv_b
    for j in range(CONV_W):
        y = y + conv_w[j] * a_ext[:, j:j + T]
    out = (jax.nn.silu(y) * b) @ w_down
    return out, a_ext[:, -(CONV_W - 1):]


def setup_inputs(seed: int = 0) -> dict:
    key = jax.random.key(seed)
    ks = jax.random.split(key, 32)
    n_pages = PAST_LEN // PAGE_SIZE
    n_used = DEC_BATCH * n_pages
    n_pool = (5 * n_used + 3) // 4
    nrm = jax.random.normal
    f32 = jnp.float32
    page_table = jax.random.permutation(ks[0], n_pool)[:n_used].reshape(DEC_BATCH, n_pages).astype(jnp.int32)
    win = [nrm(ks[6 + g], (N_LAYERS_B, DEC_BATCH, min(WINDOWS[g], PAST_LEN), 2, HEADS_PER_GROUP_B, HEAD_DIM_B), f32)
           for g in range(N_GROUPS_B)]
    qkv_b = 3 * N_HEADS_B * HEAD_DIM_B
    return {
        "x_prompt": nrm(ks[1], (BATCH, SEQ, D_MODEL), f32),
        "x_sample": nrm(ks[2], (DEC_BATCH, DEC_SEQ, D_MODEL), f32),
        "cache_moba_k": nrm(ks[3], (N_LAYERS_A, n_pool, PAGE_SIZE, N_HEADS_A, HEAD_DIM_A), f32),
        "cache_moba_v": nrm(ks[4], (N_LAYERS_A, n_pool, PAGE_SIZE, N_HEADS_A, HEAD_DIM_A), f32),
        "page_table": page_table,
        "state_win1": win[0],
        "state_win2": win[1],
        "state_win3": win[2],
        "state_conv": nrm(ks[5], (DEPTH, DEC_BATCH, CONV_W - 1, D_FF), f32),
        "c_prompt": nrm(ks[10], (BATCH, D_MODEL), f32),
        "c_sample": nrm(ks[11], (DEC_BATCH, D_MODEL), f32),
        "ada_w": nrm(ks[12], (DEPTH, D_MODEL, 6 * D_MODEL), f32) * D_MODEL ** -0.5,
        "ada_b": nrm(ks[13], (DEPTH, 6 * D_MODEL), f32) * 0.02,
        "norm1_g": 1.0 + 0.02 * nrm(ks[14], (DEPTH, D_MODEL), f32),
        "norm2_g": 1.0 + 0.02 * nrm(ks[15], (DEPTH, D_MODEL), f32),
        "final_g": 1.0 + 0.02 * nrm(ks[16], (D_MODEL,), f32),
        "a_w_qkv": nrm(ks[17], (N_LAYERS_A, D_MODEL, 3 * D_A), f32) * D_MODEL ** -0.5,
        "a_w_o": nrm(ks[18], (N_LAYERS_A, D_A, D_MODEL), f32) * D_A ** -0.5,
        "b_w_qkv": nrm(ks[19], (N_LAYERS_B, D_MODEL, qkv_b), f32) * D_MODEL ** -0.5,
        "b_w_o": nrm(ks[20], (N_LAYERS_B, D_B_OUT, D_MODEL), f32) * D_B_OUT ** -0.5,
        "ffn_w_in": nrm(ks[21], (DEPTH, D_MODEL, 2 * D_FF), f32) * D_MODEL ** -0.5,
        "ffn_conv_w": nrm(ks[22], (DEPTH, CONV_W, D_FF), f32) * CONV_W ** -0.5,
        "ffn_conv_b": nrm(ks[23], (DEPTH, D_FF), f32) * 0.02,
        "ffn_w_down": nrm(ks[24], (DEPTH, D_FF, D_MODEL), f32) * D_FF ** -0.5,
    }


def reference(x_prompt, x_sample, cache_moba_k, cache_moba_v, page_table, state_win1, state_win2,
              state_win3, state_conv, c_prompt, c_sample, ada_w, ada_b, norm1_g, norm2_g, final_g,
              a_w_qkv, a_w_o, b_w_qkv, b_w_o, ffn_w_in, ffn_conv_w, ffn_conv_b, ffn_w_down):
    slopes_a = _alibi_slopes(N_HEADS_A)
    slopes_b = _alibi_slopes(N_HEADS_B).reshape(N_GROUPS_B, HEADS_PER_GROUP_B)
    xp, xs = x_prompt, x_sample
    ka_p, va_p, ka_s, va_s = [], [], [], []
    win_p = [[] for _ in range(N_GROUPS_B)]
    win_s = [[] for _ in range(N_GROUPS_B)]
    conv_p, conv_s = [], []
    for i in range(DEPTH):
        sh1p, sc1p, g1p, sh2p, sc2p, g2p = _ada(c_prompt, ada_w[i], ada_b[i])
        sh1s, sc1s, g1s, sh2s, sc2s, g2s = _ada(c_sample, ada_w[i], ada_b[i])
        hp = _modulate(_rmsnorm(xp, norm1_g[i]), sh1p, sc1p)
        hs = _modulate(_rmsnorm(xs, norm1_g[i]), sh1s, sc1s)
        j = i // N_MIXERS
        if i % N_MIXERS == 0:
            op, kp, vp = _moba_prompt(hp, a_w_qkv[j], a_w_o[j], slopes_a)
            os_, ksm, vsm = _moba_sample(hs, cache_moba_k[j], cache_moba_v[j], page_table,
                                         a_w_qkv[j], a_w_o[j], slopes_a)
            ka_p.append(kp); va_p.append(vp); ka_s.append(ksm); va_s.append(vsm)
        else:
            op, newp = _dilated_prompt(hp, b_w_qkv[j], b_w_o[j], slopes_b)
            os_, news = _dilated_sample(hs, [state_win1[j], state_win2[j], state_win3[j]],
                                        b_w_qkv[j], b_w_o[j], slopes_b)
            for g in range(N_GROUPS_B):
                win_p[g].append(newp[g]); win_s[g].append(news[g])
        xp = xp + g1p[:, None, :] * op
        xs = xs + g1s[:, None, :] * os_
        hp = _modulate(_rmsnorm(xp, norm2_g[i]), sh2p, sc2p)
        hs = _modulate(_rmsnorm(xs, norm2_g[i]), sh2s, sc2s)
        buf0 = jnp.zeros((xp.shape[0], CONV_W - 1, D_FF), dtype=hp.dtype)
        fp, bp = _conv_ffn(hp, buf0, ffn_w_in[i], ffn_conv_w[i], ffn_conv_b[i], ffn_w_down[i])
        fs, bs = _conv_ffn(hs, state_conv[i], ffn_w_in[i], ffn_conv_w[i], ffn_conv_b[i], ffn_w_down[i])
        conv_p.append(bp); conv_s.append(bs)
        xp = xp + g2p[:, None, :] * fp
        xs = xs + g2s[:, None, :] * fs
    y_prompt = _rmsnorm(xp, final_g)
    y_sample = _rmsnorm(xs, final_g)
    moba_k_prompt = jnp.stack(ka_p)
    moba_v_prompt = jnp.stack(va_p)
    moba_k_sample = jnp.stack(ka_s)
    moba_v_sample = jnp.stack(va_s)
    win1_prompt = jnp.stack(win_p[0])
    win2_prompt = jnp.stack(win_p[1])
    win3_prompt = jnp.stack(win_p[2])
    win1_sample = jnp.stack(win_s[0])
    win2_sample = jnp.stack(win_s[1])
    win3_sample = jnp.stack(win_s[2])
    conv_prompt = jnp.stack(conv_p)
    conv_sample = jnp.stack(conv_s)
    return (y_prompt, y_sample, moba_k_prompt, moba_v_prompt, moba_k_sample, moba_v_sample,
            win1_prompt, win2_prompt, win3_prompt, win1_sample, win2_sample, win3_sample,
            conv_prompt, conv_sample)
```

```python
import functools

import jax
import jax.numpy as jnp
from jax import lax
from jax.experimental import pallas as pl
from jax.experimental.pallas import tpu as pltpu

F32 = jnp.float32
BF16 = jnp.bfloat16

HEAD_DIM_A = 64
MOBA_BLOCK = 256
MOBA_TOPK = 3
HEAD_DIM_B = 128
N_GROUPS_B = 3
HEADS_PER_GROUP_B = 4
D_GROUP_B = HEADS_PER_GROUP_B * HEAD_DIM_B
WINDOWS = (128, 512, 2048)
DILATIONS = (1, 4, 16)
CONV_W = 3
RMS_EPS = 1e-6
NEG_INF = -1e30
POS_BIG = 1e30

LANES = 128
SUBLANES = 8
VMEM_LIMIT = 56 * 1024 * 1024

_NT = (((1,), (1,)), ((), ()))


def _cparams(*sem):
    return pltpu.CompilerParams(dimension_semantics=sem, vmem_limit_bytes=VMEM_LIMIT)


def _row_tile(t, cap=512):
    tm = min(t, cap)
    assert t % tm == 0
    return tm


def _norm_mod(x, g, sh, sc):
    ms = jnp.mean(x * x, axis=-1, keepdims=True)
    y = x * lax.rsqrt(ms + RMS_EPS) * g
    return y * (1.0 + sc) + sh


def _ada_body(c_ref, w_ref, b_ref, o_ref):
    c = c_ref[...]
    s = (c * jax.nn.sigmoid(c)).astype(BF16)
    o_ref[0] = jnp.dot(s, w_ref[0].astype(BF16), preferred_element_type=F32) + b_ref[0]


def _ada_all(c_all, ada_w, ada_b):
    depth, d, n6 = ada_w.shape
    nb = c_all.shape[0]
    tn = n6 // 4
    return pl.pallas_call(
        _ada_body,
        grid=(depth, n6 // tn),
        in_specs=[pl.BlockSpec((nb, d), lambda l, n: (0, 0)),
                  pl.BlockSpec((1, d, tn), lambda l, n: (l, 0, n)),
                  pl.BlockSpec((1, 1, tn), lambda l, n: (l, 0, n))],
        out_specs=pl.BlockSpec((1, nb, tn), lambda l, n: (l, 0, n)),
        out_shape=jax.ShapeDtypeStruct((depth, nb, n6), F32),
        compiler_params=_cparams("parallel", "parallel"),
        name="ada_table",
    )(c_all, ada_w, ada_b.reshape(depth, 1, n6))


def _nm_matmul_body(x_ref, sh_ref, sc_ref, g_ref, w_ref, o_ref, h_s):
    @pl.when(pl.program_id(2) == 0)
    def _():
        h_s[...] = _norm_mod(x_ref[0], g_ref[...], sh_ref[0], sc_ref[0]).astype(BF16)

    o_ref[0] = jnp.dot(h_s[...], w_ref[...], preferred_element_type=F32)


def _nm_matmul(x, sh, sc, g, w, n_tiles=3):
    b, t, d = x.shape
    n = w.shape[1]
    tm = _row_tile(t)
    tmod = sh.shape[1]
    tmb = tm if tmod == t else 1
    mod_map = (lambda bi, mi, ni: (bi, mi, 0)) if tmod == t else (lambda bi, mi, ni: (bi, 0, 0))
    tn = n // n_tiles
    return pl.pallas_call(
        _nm_matmul_body,
        grid=(b, t // tm, n_tiles),
        in_specs=[pl.BlockSpec((1, tm, d), lambda bi, mi, ni: (bi, mi, 0)),
                  pl.BlockSpec((1, tmb, d), mod_map),
                  pl.BlockSpec((1, tmb, d), mod_map),
                  pl.BlockSpec((1, d), lambda bi, mi, ni: (0, 0)),
                  pl.BlockSpec((d, tn), lambda bi, mi, ni: (0, ni))],
        out_specs=pl.BlockSpec((1, tm, tn), lambda bi, mi, ni: (bi, mi, ni)),
        out_shape=jax.ShapeDtypeStruct((b, t, n), F32),
        scratch_shapes=[pltpu.VMEM((tm, d), BF16)],
        compiler_params=_cparams("parallel", "parallel", "arbitrary"),
        name="norm_mod_matmul",
    )(x, sh, sc, g, w)


def _topk_rank_select(gate, blk, n_valid, n_rows):
    gate = jnp.where(blk < n_valid, gate, NEG_INF)
    rank = jnp.zeros(gate.shape, F32)
    for m in range(n_rows):
        gm = gate[m:m + 1, :]
        beats = (gm > gate) | ((gm == gate) & (blk > m))
        rank = rank + jnp.where(beats, 1.0, 0.0)
    n_sel = jnp.minimum(n_valid, MOBA_TOPK).astype(F32)
    return jnp.where((rank < n_sel) & (blk < n_valid), 1.0, 0.0)


def _moba_prompt_body(slopes_ref, q_ref, k_ref, v_ref, o_ref, kb_s, vt_s, km_s, d0_s, sel_s, *, nb):
    blk_sz, hd = MOBA_BLOCK, HEAD_DIM_A
    pair = pl.program_id(1)
    i = pl.program_id(2)

    @pl.when(i == 0)
    def _init():
        for n in range(nb):
            kblk = k_ref[0, n * blk_sz:(n + 1) * blk_sz, :]
            kb_s[n] = kblk.astype(BF16)
            km_s[n:n + 1, :] = jnp.sum(kblk, axis=0, keepdims=True) * (1.0 / blk_sz)
            vt_s[n] = v_ref[0, n * blk_sz:(n + 1) * blk_sz, :].T.astype(BF16)
        rk = lax.broadcasted_iota(jnp.int32, (blk_sz, blk_sz), 0)
        rq = lax.broadcasted_iota(jnp.int32, (blk_sz, blk_sz), 1)
        rel = (rk - rq).astype(F32)
        for hh in range(2):
            d0 = slopes_ref[2 * pair + hh] * rel
            d0_s[hh] = d0
            d0_s[2 + hh] = jnp.where(rk <= rq, d0, NEG_INF)

    q = q_ref[0] * (hd ** -0.5)
    lane = lax.broadcasted_iota(jnp.int32, (1, LANES), 1)
    blk = lax.broadcasted_iota(jnp.int32, (nb, blk_sz), 0)
    km = km_s[...]
    qb = []
    for hh in range(2):
        qh = jnp.where((lane >= hh * hd) & (lane < (hh + 1) * hd), q, 0.0)
        gate = lax.dot_general(km, qh, _NT, precision=lax.Precision.HIGHEST,
                               preferred_element_type=F32)
        sel_s[hh] = _topk_rank_select(gate, blk, i, nb)
        qb.append(qh.astype(BF16))
    slopes = [slopes_ref[2 * pair + hh] for hh in range(2)]

    def head_update(hh, n, state, d0, c_n, selrow):
        m_old, l_old, acc = state
        s0 = lax.dot_general(kb_s[n], qb[hh], _NT, preferred_element_type=F32) + d0
        cmax = jnp.max(s0, axis=0, keepdims=True)
        if selrow is None:
            m_new = jnp.maximum(m_old, cmax)
            shift = m_new
        else:
            m_new = jnp.maximum(m_old, jnp.where(selrow, cmax + c_n, NEG_INF))
            shift = jnp.where(selrow, m_new - c_n, POS_BIG)
        alpha = jnp.exp(m_old - m_new)
        p = jnp.exp(s0 - shift)
        l_new = alpha * l_old + jnp.sum(p, axis=0, keepdims=True)
        vt = vt_s[n][hh * hd:(hh + 1) * hd, :]
        acc_new = alpha * acc + jnp.dot(vt, p.astype(BF16), preferred_element_type=F32)
        return m_new, l_new, acc_new

    def past_block(n, carry):
        off = ((i - n) * blk_sz).astype(F32)
        out = []
        for hh in range(2):
            selrow = sel_s[hh, pl.ds(n, 1), :] > 0.5
            out.append(head_update(hh, n, carry[hh], d0_s[hh], -slopes[hh] * off, selrow))
        return tuple(out)

    init = tuple((jnp.full((1, blk_sz), NEG_INF, F32), jnp.zeros((1, blk_sz), F32),
                  jnp.zeros((hd, blk_sz), F32)) for _ in range(2))
    carry = lax.fori_loop(0, i, past_block, init)
    outs = []
    for hh in range(2):
        _, l_fin, acc = head_update(hh, i, carry[hh], d0_s[2 + hh], None, None)
        outs.append(acc / l_fin)
    o_ref[0] = jnp.concatenate(outs, axis=0).T.astype(o_ref.dtype)


def _moba_prompt_attn(qkv, slopes):
    b, t, d3 = qkv.shape
    d = d3 // 3
    nb = t // MOBA_BLOCK
    n_pairs = d // LANES
    assert t % MOBA_BLOCK == 0 and d % LANES == 0
    return pl.pallas_call(
        functools.partial(_moba_prompt_body, nb=nb),
        grid=(b, n_pairs, nb),
        in_specs=[pl.BlockSpec(memory_space=pltpu.SMEM),
                  pl.BlockSpec((1, MOBA_BLOCK, LANES), lambda bi, p, i: (bi, i, p)),
                  pl.BlockSpec((1, t, LANES), lambda bi, p, i: (bi, 0, n_pairs + p)),
                  pl.BlockSpec((1, t, LANES), lambda bi, p, i: (bi, 0, 2 * n_pairs + p))],
        out_specs=pl.BlockSpec((1, MOBA_BLOCK, LANES), lambda bi, p, i: (bi, i, p)),
        out_shape=jax.ShapeDtypeStruct((b, t, d), BF16),
        scratch_shapes=[pltpu.VMEM((nb, MOBA_BLOCK, LANES), BF16),
                        pltpu.VMEM((nb, LANES, MOBA_BLOCK), BF16),
                        pltpu.VMEM((nb, LANES), F32),
                        pltpu.VMEM((4, MOBA_BLOCK, MOBA_BLOCK), F32),
                        pltpu.VMEM((2, nb, MOBA_BLOCK), F32)],
        compiler_params=_cparams("parallel", "parallel", "arbitrary"),
        name="moba_prompt_attn",
    )(slopes, qkv, qkv, qkv)


def _head_block_diag(q, n_heads, hd):
    d = n_heads * hd
    n_t = LANES // n_heads
    row_h = lax.broadcasted_iota(jnp.int32, (n_heads, d), 0)
    col_h = lax.broadcasted_iota(jnp.int32, (n_heads, d), 1) // hd
    pieces = []
    for t in range(n_t):
        if t < q.shape[0]:
            pieces.append(jnp.where(row_h == col_h, jnp.broadcast_to(q[t:t + 1, :], (n_heads, d)), 0.0))
        else:
            pieces.append(jnp.zeros((n_heads, d), F32))
    return jnp.concatenate(pieces, axis=0)


def _moba_s_scores_body(pt_ref, q_ref, slope_ref, tq_ref, k0_ref, k1_ref, s_ref, km_ref, qbd_s,
                        *, n_heads, past_len):
    n = pl.program_id(1)
    half = MOBA_BLOCK // 2

    @pl.when(n == 0)
    def _():
        qbd_s[...] = _head_block_diag(q_ref[0] * (HEAD_DIM_A ** -0.5), n_heads, HEAD_DIM_A).astype(BF16)

    qbd = qbd_s[...]
    rk = lax.broadcasted_iota(jnp.int32, (half, LANES), 0)
    tot = jnp.zeros((1, k0_ref.shape[-1]), F32)
    for part, k_ref in enumerate((k0_ref, k1_ref)):
        kp = k_ref[0, 0]
        tot = tot + jnp.sum(kp, axis=0, keepdims=True)
        s = lax.dot_general(kp.astype(BF16), qbd, _NT, preferred_element_type=F32)
        kpos = n * MOBA_BLOCK + part * half + rk
        dist = (past_len + tq_ref[...] - kpos).astype(F32)
        s_ref[0, 0, part * half:(part + 1) * half, :] = s - slope_ref[...] * dist
    km_ref[0, pl.ds(n, 1), :] = tot * (1.0 / MOBA_BLOCK)


def _moba_s_softmax_body(s_ref, km_ref, q_ref, kn_ref, slope_ref, tq_ref, pt_ref, po_ref, l_ref,
                         sel_s, lacc_s, *, n_heads, n_blocks, t_dec):
    qbd = _head_block_diag(q_ref[0] * (HEAD_DIM_A ** -0.5), n_heads, HEAD_DIM_A)
    gate = lax.dot_general(km_ref[0], qbd, _NT, precision=lax.Precision.HIGHEST,
                           preferred_element_type=F32)
    blk = lax.broadcasted_iota(jnp.int32, gate.shape, 0)
    sel_s[...] = _topk_rank_select(gate, blk, jnp.int32(n_blocks), n_blocks)

    kn = kn_ref[0]
    s_own = lax.dot_general(kn.astype(BF16), qbd.astype(BF16), _NT, preferred_element_type=F32)
    tk = lax.broadcasted_iota(jnp.int32, s_own.shape, 0)
    d_own = tq_ref[...] - tk
    s_own = jnp.where((d_own >= 0) & (tk < t_dec), s_own - slope_ref[...] * d_own.astype(F32), NEG_INF)

    def max_body(n, m):
        cmax = jnp.max(s_ref[0, n], axis=0, keepdims=True)
        return jnp.maximum(m, jnp.where(sel_s[pl.ds(n, 1), :] > 0.5, cmax, NEG_INF))

    m = lax.fori_loop(0, n_blocks, max_body, jnp.max(s_own, axis=0, keepdims=True))

    lacc_s[...] = jnp.zeros(lacc_s.shape, F32)

    def p_body(n, carry):
        shift = jnp.where(sel_s[pl.ds(n, 1), :] > 0.5, m, POS_BIG)
        p_t = jnp.exp(s_ref[0, n] - shift).T
        lacc_s[...] += p_t
        pt_ref[0, n] = p_t.astype(BF16)
        return carry

    lax.fori_loop(0, n_blocks, p_body, 0)
    p_own = jnp.exp(s_own - m)
    p_own_t = jnp.concatenate([p_own, jnp.zeros((LANES - p_own.shape[0], LANES), F32)], axis=0).T
    po_ref[0] = p_own_t
    l = jnp.sum(lacc_s[...], axis=1, keepdims=True) + jnp.sum(p_own_t, axis=1, keepdims=True)
    l_ref[0] = jnp.broadcast_to(l, (LANES, LANES))


def _moba_s_pv_body(ptab_ref, p_ref, po_ref, l_ref, vn_ref, v0_ref, v1_ref, o_ref, acc_s,
                    *, n_heads, t_dec):
    n = pl.program_id(1)
    half = MOBA_BLOCK // 2
    hd = HEAD_DIM_A

    @pl.when(n == 0)
    def _():
        po = po_ref[0]
        vn = vn_ref[0]
        acc = jnp.zeros(acc_s.shape, F32)
        for t in range(t_dec):
            acc = acc + po[:, t:t + 1] * vn[t:t + 1, :]
        acc_s[...] = acc

    p = p_ref[0, 0]
    acc_s[...] += (jnp.dot(p[:, :half], v0_ref[0, 0].astype(BF16), preferred_element_type=F32)
                   + jnp.dot(p[:, half:], v1_ref[0, 0].astype(BF16), preferred_element_type=F32))

    @pl.when(n == pl.num_programs(1) - 1)
    def _():
        d = acc_s.shape[1]
        row_h = lax.broadcasted_iota(jnp.int32, (LANES, d), 0) % n_heads
        col_h = lax.broadcasted_iota(jnp.int32, (LANES, d), 1) // hd
        accn = jnp.where(row_h == col_h, acc_s[...] / l_ref[0][:, 0:1], 0.0)
        for t in range(t_dec):
            o_ref[0, t:t + 1, :] = jnp.sum(accn[t * n_heads:(t + 1) * n_heads, :], axis=0, keepdims=True)


def _moba_sample_attn(qkv_s, cache_k, cache_v, layer, page_table, slope_cols, tq_cols):
    bs, t_dec, d3 = qkv_s.shape
    d = d3 // 3
    n_heads = d // HEAD_DIM_A
    page = cache_k.shape[2]
    n_pages = page_table.shape[1]
    past_len = n_pages * page
    assert MOBA_BLOCK == 2 * page and past_len % MOBA_BLOCK == 0
    assert n_heads * t_dec <= LANES and t_dec <= SUBLANES
    n_blocks = past_len // MOBA_BLOCK
    assert n_blocks >= MOBA_TOPK
    pad = SUBLANES - t_dec
    q8 = jnp.pad(qkv_s[:, :, :d], ((0, 0), (0, pad), (0, 0)))
    k8 = jnp.pad(qkv_s[:, :, d:2 * d], ((0, 0), (0, pad), (0, 0)))
    v8 = jnp.pad(qkv_s[:, :, 2 * d:], ((0, 0), (0, pad), (0, 0)))

    def page_map(part):
        return lambda bi, n, pt: (layer, pt[bi, 2 * n + part], 0, 0)

    small = lambda bi, n, pt: (bi, 0, 0)
    cols = lambda bi, n, pt: (0, 0)
    s_all, k_mean = pl.pallas_call(
        functools.partial(_moba_s_scores_body, n_heads=n_heads, past_len=past_len),
        grid_spec=pltpu.PrefetchScalarGridSpec(
            num_scalar_prefetch=1,
            grid=(bs, n_blocks),
            in_specs=[pl.BlockSpec((1, SUBLANES, d), small),
                      pl.BlockSpec((1, LANES), cols),
                      pl.BlockSpec((1, LANES), cols),
                      pl.BlockSpec((1, 1, page, d), page_map(0)),
                      pl.BlockSpec((1, 1, page, d), page_map(1))],
            out_specs=[pl.BlockSpec((1, 1, MOBA_BLOCK, LANES), lambda bi, n, pt: (bi, n, 0, 0)),
                       pl.BlockSpec((1, n_blocks, d), small)],
            scratch_shapes=[pltpu.VMEM((LANES, d), BF16)]),
        out_shape=[jax.ShapeDtypeStruct((bs, n_blocks, MOBA_BLOCK, LANES), F32),
                   jax.ShapeDtypeStruct((bs, n_blocks, d), F32)],
        compiler_params=_cparams("parallel", "arbitrary"),
        name="moba_sample_scores",
    )(page_table, q8, slope_cols, tq_cols, cache_k, cache_k)

    p_t, p_own, l_sum = pl.pallas_call(
        functools.partial(_moba_s_softmax_body, n_heads=n_heads, n_blocks=n_blocks, t_dec=t_dec),
        grid=(bs,),
        in_specs=[pl.BlockSpec((1, n_blocks, MOBA_BLOCK, LANES), lambda bi: (bi, 0, 0, 0)),
                  pl.BlockSpec((1, n_blocks, d), lambda bi: (bi, 0, 0)),
                  pl.BlockSpec((1, SUBLANES, d), lambda bi: (bi, 0, 0)),
                  pl.BlockSpec((1, SUBLANES, d), lambda bi: (bi, 0, 0)),
                  pl.BlockSpec((1, LANES), lambda bi: (0, 0)),
                  pl.BlockSpec((1, LANES), lambda bi: (0, 0))],
        out_specs=[pl.BlockSpec((1, n_blocks, LANES, MOBA_BLOCK), lambda bi: (bi, 0, 0, 0)),
                   pl.BlockSpec((1, LANES, LANES), lambda bi: (bi, 0, 0)),
                   pl.BlockSpec((1, LANES, LANES), lambda bi: (bi, 0, 0))],
        out_shape=[jax.ShapeDtypeStruct((bs, n_blocks, LANES, MOBA_BLOCK), BF16),
                   jax.ShapeDtypeStruct((bs, LANES, LANES), F32),
                   jax.ShapeDtypeStruct((bs, LANES, LANES), F32)],
        scratch_shapes=[pltpu.VMEM((n_blocks, LANES), F32),
                        pltpu.VMEM((LANES, MOBA_BLOCK), F32)],
        compiler_params=_cparams("parallel"),
        name="moba_sample_softmax",
    )(s_all, k_mean, q8, k8, slope_cols, tq_cols)

    return pl.pallas_call(
        functools.partial(_moba_s_pv_body, n_heads=n_heads, t_dec=t_dec),
        grid_spec=pltpu.PrefetchScalarGridSpec(
            num_scalar_prefetch=1,
            grid=(bs, n_blocks),
            in_specs=[pl.BlockSpec((1, 1, LANES, MOBA_BLOCK), lambda bi, n, pt: (bi, n, 0, 0)),
                      pl.BlockSpec((1, LANES, LANES), small),
                      pl.BlockSpec((1, LANES, LANES), small),
                      pl.BlockSpec((1, SUBLANES, d), small),
                      pl.BlockSpec((1, 1, page, d), page_map(0)),
                      pl.BlockSpec((1, 1, page, d), page_map(1))],
            out_specs=pl.BlockSpec((1, t_dec, d), small),
            scratch_shapes=[pltpu.VMEM((LANES, d), F32)]),
        out_shape=jax.ShapeDtypeStruct((bs, t_dec, d), F32),
        compiler_params=_cparams("parallel", "arbitrary"),
        name="moba_sample_pv",
    )(page_table, p_t, p_own, l_sum, v8, cache_v, cache_v)


def _dilated_prompt_body(slopes_ref, q_ref, kp_ref, kc_ref, vp_ref, vc_ref, o_ref, lse_ref, bias_s,
                         *, group, dilation, n_keys):
    tq = q_ref.shape[1]
    hd = HEAD_DIM_B
    first = (pl.program_id(0) == 0) & (pl.program_id(1) == 0) & (pl.program_id(2) == 0)

    @pl.when(first)
    def _():
        iq = lax.broadcasted_iota(jnp.int32, (tq, 2 * tq), 0)
        jk = lax.broadcasted_iota(jnp.int32, (tq, 2 * tq), 1)
        steps = iq + tq - jk
        ok = (steps >= 0) & (steps < n_keys)
        dist = (steps * dilation).astype(F32)
        for h in range(HEADS_PER_GROUP_B):
            b = -slopes_ref[group * HEADS_PER_GROUP_B + h] * dist
            bias_s[0, h] = jnp.where(ok & (jk >= tq), b, NEG_INF)
            bias_s[1, h] = jnp.where(ok, b, NEG_INF)

    var = jnp.minimum(pl.program_id(2), 1)
    scale = hd ** -0.5
    for h in range(HEADS_PER_GROUP_B):
        cs = slice(h * hd, (h + 1) * hd)
        qh = q_ref[0, :, cs].astype(BF16)
        bias = bias_s[var, h]
        s_p = lax.dot_general(qh, kp_ref[0, :, cs].astype(BF16), _NT, preferred_element_type=F32) * scale + bias[:, :tq]
        s_c = lax.dot_general(qh, kc_ref[0, :, cs].astype(BF16), _NT, preferred_element_type=F32) * scale + bias[:, tq:]
        m = jnp.maximum(jnp.max(s_p, axis=1, keepdims=True), jnp.max(s_c, axis=1, keepdims=True))
        p_p = jnp.exp(s_p - m)
        p_c = jnp.exp(s_c - m)
        l = jnp.sum(p_p, axis=1, keepdims=True) + jnp.sum(p_c, axis=1, keepdims=True)
        o = (jnp.dot(p_p.astype(BF16), vp_ref[0, :, cs].astype(BF16), preferred_element_type=F32)
             + jnp.dot(p_c.astype(BF16), vc_ref[0, :, cs].astype(BF16), preferred_element_type=F32))
        o_ref[0, :, cs] = o / l
        lse_ref[0, :, cs] = jnp.broadcast_to(m + jnp.log(l), (tq, hd))


def _dilated_prompt_group(qkv, slopes, group):
    b, t, n_all = qkv.shape
    dg = D_GROUP_B
    dil = DILATIONS[group]
    n_keys = WINDOWS[group] // dil + 1
    tq = n_keys - 1
    tr = t // dil
    assert t % dil == 0 and tr % tq == 0 and tq % LANES == 0
    cols = n_all // dg
    view = qkv.reshape(b, tr, dil * n_all)

    def sec(section, prev):
        def index(bi, r, ti):
            row = jnp.maximum(ti - 1, 0) if prev else ti
            return (bi, row, r * cols + section * N_GROUPS_B + group)
        return pl.BlockSpec((1, tq, dg), index)

    out_spec = pl.BlockSpec((1, tq, dg), lambda bi, r, ti: (bi, ti, r))
    o, lse = pl.pallas_call(
        functools.partial(_dilated_prompt_body, group=group, dilation=dil, n_keys=n_keys),
        grid=(b, dil, tr // tq),
        in_specs=[pl.BlockSpec(memory_space=pltpu.SMEM),
                  sec(0, False), sec(1, True), sec(1, False), sec(2, True), sec(2, False)],
        out_specs=[out_spec, out_spec],
        out_shape=[jax.ShapeDtypeStruct((b, tr, dil * dg), F32)] * 2,
        scratch_shapes=[pltpu.VMEM((2, HEADS_PER_GROUP_B, tq, 2 * tq), F32)],
        compiler_params=_cparams("arbitrary", "arbitrary", "arbitrary"),
        name=f"dilated_prompt_g{group}",
    )(slopes, view, view, view, view, view)
    return o.reshape(b, t, dg), lse.reshape(b, t, dg)


def _dilated_sample_body(slopes_ref, x_ref, b0_ref, b1_ref, b2_ref, o_ref, *, t_dec):
    hd = HEAD_DIM_B
    dg = D_GROUP_B
    scale = hd ** -0.5
    x = x_ref[0]
    rows = x.shape[0]
    tq = lax.broadcasted_iota(jnp.int32, (rows, LANES), 0)
    mcol = lax.broadcasted_iota(jnp.int32, (rows, LANES), 1)
    tq1 = lax.broadcasted_iota(jnp.int32, (rows, 1), 0)
    bufs = (b0_ref, b1_ref, b2_ref)
    for h in range(HEADS_PER_GROUP_B):
        outs, lses = [], []
        for g in range(N_GROUPS_B):
            dil = DILATIONS[g]
            n_back = WINDOWS[g] // dil
            slope = slopes_ref[g * HEADS_PER_GROUP_B + h]
            c0 = g * dg + h * hd
            qh = x[:, c0:c0 + hd]
            kn = x[:, N_GROUPS_B * dg + c0:N_GROUPS_B * dg + c0 + hd]
            vn = x[:, 2 * N_GROUPS_B * dg + c0:2 * N_GROUPS_B * dg + c0 + hd]
            qb = qh.astype(BF16)
            s_list, v_list = [], []
            for r in range(min(dil, t_dec)):
                base = r * 2 * dg + h * hd
                kr = bufs[g][0, 0, :, base:base + hd]
                v_list.append(bufs[g][0, 0, :, base + dg:base + dg + hd])
                num = dil * (n_back - mcol) + (tq - r)
                ok = (num >= 0) & (num <= dil * n_back) & ((num & (dil - 1)) == 0)
                s = lax.dot_general(qb, kr.astype(BF16), _NT, preferred_element_type=F32) * scale
                s_list.append(jnp.where(ok, s - slope * num.astype(F32), NEG_INF))
            s_new = []
            for tk in range(t_dec):
                num = tq1 - tk
                ok = (num >= 0) & (num <= dil * n_back) & ((num & (dil - 1)) == 0)
                s = jnp.sum(qh * kn[tk:tk + 1, :], axis=1, keepdims=True) * scale
                s_new.append(jnp.where(ok, s - slope * num.astype(F32), NEG_INF))
            m = s_new[0]
            for s in s_new[1:]:
                m = jnp.maximum(m, s)
            for s in s_list:
                m = jnp.maximum(m, jnp.max(s, axis=1, keepdims=True))
            l = jnp.zeros((rows, 1), F32)
            o = jnp.zeros((rows, hd), F32)
            for s, v in zip(s_list, v_list):
                p = jnp.exp(s - m)
                l = l + jnp.sum(p, axis=1, keepdims=True)
                o = o + jnp.dot(p.astype(BF16), v.astype(BF16), preferred_element_type=F32)
            for tk, s in enumerate(s_new):
                p = jnp.exp(s - m)
                l = l + p
                o = o + p * vn[tk:tk + 1, :]
            outs.append(o / l)
            lses.append(m + jnp.log(l))
        lm = jnp.maximum(jnp.maximum(lses[0], lses[1]), lses[2])
        es = [jnp.exp(ls - lm) for ls in lses]
        den = es[0] + es[1] + es[2]
        merged = (es[0] * outs[0] + es[1] * outs[1] + es[2] * outs[2]) / den
        o_ref[0, :, h * hd:(h + 1) * hd] = merged[:t_dec, :]


def _dilated_sample_attn(qkv_s, states, layer, slopes):
    bs, t_dec, n_all = qkv_s.shape
    dg = D_GROUP_B
    x8 = jnp.pad(qkv_s, ((0, 0), (0, SUBLANES - t_dec), (0, 0)))
    views, specs = [], []
    for g in range(N_GROUPS_B):
        dil = DILATIONS[g]
        n_l, _, wb = states[g].shape[:3]
        n_back = WINDOWS[g] // dil
        assert wb == WINDOWS[g] == n_back * dil and n_back == LANES
        assert dil & (dil - 1) == 0 and (dil == 1 or dil >= t_dec)
        views.append(states[g].reshape(n_l, bs, n_back, dil * 2 * dg))
        specs.append(pl.BlockSpec((1, 1, n_back, min(dil, t_dec) * 2 * dg), lambda bi: (layer, bi, 0, 0)))
    return pl.pallas_call(
        functools.partial(_dilated_sample_body, t_dec=t_dec),
        grid=(bs,),
        in_specs=[pl.BlockSpec(memory_space=pltpu.SMEM),
                  pl.BlockSpec((1, SUBLANES, n_all), lambda bi: (bi, 0, 0))] + specs,
        out_specs=pl.BlockSpec((1, t_dec, dg), lambda bi: (bi, 0, 0)),
        out_shape=jax.ShapeDtypeStruct((bs, t_dec, dg), F32),
        compiler_params=_cparams("parallel"),
        name="dilated_sample_attn",
    )(slopes, x8, *views)


def _oproj_body(o_ref, w_ref, x_ref, g_ref, out_ref):
    y = jnp.dot(o_ref[0].astype(BF16), w_ref[...], preferred_element_type=F32)
    out_ref[0] = x_ref[0] + g_ref[0] * y


def _oproj_merge_body(o0_ref, o1_ref, o2_ref, l0_ref, l1_ref, l2_ref, w_ref, x_ref, g_ref, out_ref):
    l0, l1, l2 = l0_ref[0], l1_ref[0], l2_ref[0]
    lm = jnp.maximum(jnp.maximum(l0, l1), l2)
    e0, e1, e2 = jnp.exp(l0 - lm), jnp.exp(l1 - lm), jnp.exp(l2 - lm)
    o = (e0 * o0_ref[0] + e1 * o1_ref[0] + e2 * o2_ref[0]) / (e0 + e1 + e2)
    y = jnp.dot(o.astype(BF16), w_ref[...], preferred_element_type=F32)
    out_ref[0] = x_ref[0] + g_ref[0] * y


def _oproj(o_list, w, x, gate):
    b, t, d = x.shape
    din = w.shape[0]
    tm = _row_tile(t)
    tmod = gate.shape[1]
    tmb = tm if tmod == t else 1
    gate_map = (lambda bi, mi: (bi, mi, 0)) if tmod == t else (lambda bi, mi: (bi, 0, 0))
    row = lambda bi, mi: (bi, mi, 0)
    body = _oproj_body if len(o_list) == 1 else _oproj_merge_body
    return pl.pallas_call(
        body,
        grid=(b, t // tm),
        in_specs=[pl.BlockSpec((1, tm, din), row)] * len(o_list) + [
            pl.BlockSpec((din, d), lambda bi, mi: (0, 0)),
            pl.BlockSpec((1, tm, d), row),
            pl.BlockSpec((1, tmb, d), gate_map)],
        out_specs=pl.BlockSpec((1, tm, d), row),
        out_shape=jax.ShapeDtypeStruct((b, t, d), F32),
        compiler_params=_cparams("parallel", "parallel"),
        name="out_proj_residual",
    )(*o_list, w, x, gate)


def _conv_ffn_body(x_ref, sh_ref, sc_ref, gt_ref, g_ref, win_ref, cw_ref, cb_ref, wd_ref, p1_ref, p2_ref,
                   out_ref, a_ref, z_s, carry_s, *, chunk, seq_len, carry_rows):
    tm = x_ref.shape[1]
    dff = wd_ref.shape[0]
    x = x_ref[0]
    h = _norm_mod(x, g_ref[...], sh_ref[0], sc_ref[0]).astype(BF16)
    row = lax.broadcasted_iota(jnp.int32, (tm, chunk), 0)
    t_in_seq = row % seq_len
    if carry_rows:
        @pl.when(pl.program_id(1) == 0)
        def _():
            carry_s[...] = jnp.zeros(carry_s.shape, F32)

    for c in range(dff // chunk):
        cs = slice(c * chunk, (c + 1) * chunk)
        a = jnp.dot(h, win_ref[:, cs], preferred_element_type=F32)
        bgate = jnp.dot(h, win_ref[:, dff + c * chunk:dff + (c + 1) * chunk], preferred_element_type=F32)
        a1 = jnp.where(t_in_seq >= 1, pltpu.roll(a, 1, axis=0), 0.0)
        a2 = jnp.where(t_in_seq >= 2, pltpu.roll(a, 2, axis=0), 0.0)
        if carry_rows:
            prev = carry_s[:, cs]
            a1 = jnp.where(row == 0, prev[SUBLANES - 1:SUBLANES, :], a1)
            a2 = jnp.where(row == 0, prev[SUBLANES - 2:SUBLANES - 1, :],
                           jnp.where(row == 1, prev[SUBLANES - 1:SUBLANES, :], a2))
            carry_s[:, cs] = a[tm - SUBLANES:, :]
            a_ref[0, :, cs] = a[tm - SUBLANES:, :]
        else:
            a1 = a1 + p1_ref[0, :, cs]
            a2 = a2 + p2_ref[0, :, cs]
            a_ref[0, :, cs] = a
        y = cb_ref[:, cs] + cw_ref[0:1, cs] * a2 + cw_ref[1:2, cs] * a1 + cw_ref[2:3, cs] * a
        z_s[:, cs] = (y * jax.nn.sigmoid(y) * bgate).astype(BF16)
    f = jnp.dot(z_s[...], wd_ref[...], preferred_element_type=F32)
    out_ref[0] = x + gt_ref[0] * f


def _conv_ffn(x, sh, sc, gate, g, w_in, conv_w, conv_b, w_down, seq_len, p1=None, p2=None):
    b, t, d = x.shape
    dff = w_down.shape[0]
    tm = _row_tile(t)
    carry_rows = p1 is None
    chunk = 256
    assert dff % chunk == 0 and tm % SUBLANES == 0
    tmod = sh.shape[1]
    tmb = tm if tmod == t else 1
    mod_map = (lambda bi, mi: (bi, mi, 0)) if tmod == t else (lambda bi, mi: (bi, 0, 0))
    row = lambda bi, mi: (bi, mi, 0)
    const = lambda bi, mi: (0, 0)
    if carry_rows:
        assert seq_len == t
        p1 = p2 = jnp.zeros((1, SUBLANES, LANES), F32)
        p_spec = pl.BlockSpec((1, SUBLANES, LANES), lambda bi, mi: (0, 0, 0))
        a_rows = SUBLANES
        a_spec = pl.BlockSpec((1, SUBLANES, dff), lambda bi, mi: (bi, 0, 0))
    else:
        assert tm % seq_len == 0
        p_spec = pl.BlockSpec((1, tm, dff), row)
        a_rows = t
        a_spec = pl.BlockSpec((1, tm, dff), row)
    return pl.pallas_call(
        functools.partial(_conv_ffn_body, chunk=chunk, seq_len=seq_len, carry_rows=carry_rows),
        grid=(b, t // tm),
        in_specs=[pl.BlockSpec((1, tm, d), row),
                  pl.BlockSpec((1, tmb, d), mod_map),
                  pl.BlockSpec((1, tmb, d), mod_map),
                  pl.BlockSpec((1, tmb, d), mod_map),
                  pl.BlockSpec((1, d), const),
                  pl.BlockSpec((d, 2 * dff), const),
                  pl.BlockSpec((CONV_W, dff), const),
                  pl.BlockSpec((1, dff), const),
                  pl.BlockSpec((dff, d), const),
                  p_spec, p_spec],
        out_specs=[pl.BlockSpec((1, tm, d), row), a_spec],
        out_shape=[jax.ShapeDtypeStruct((b, t, d), F32),
                   jax.ShapeDtypeStruct((b, a_rows, dff), F32)],
        scratch_shapes=[pltpu.VMEM((tm, dff), BF16),
                        pltpu.VMEM((SUBLANES, dff), F32)],
        compiler_params=_cparams("parallel", "arbitrary"),
        name="conv_ffn",
    )(x, sh, sc, gate, g, w_in, conv_w, conv_b, w_down, p1, p2)


def _final_norm_body(x_ref, g_ref, o_ref):
    x = x_ref[0]
    ms = jnp.mean(x * x, axis=-1, keepdims=True)
    o_ref[0] = x * lax.rsqrt(ms + RMS_EPS) * g_ref[...]


def _final_norm(x, g):
    b, t, d = x.shape
    tm = _row_tile(t)
    return pl.pallas_call(
        _final_norm_body,
        grid=(b, t // tm),
        in_specs=[pl.BlockSpec((1, tm, d), lambda bi, mi: (bi, mi, 0)),
                  pl.BlockSpec((1, d), lambda bi, mi: (0, 0))],
        out_specs=pl.BlockSpec((1, tm, d), lambda bi, mi: (bi, mi, 0)),
        out_shape=jax.ShapeDtypeStruct((b, t, d), F32),
        compiler_params=_cparams("parallel", "parallel"),
        name="final_norm",
    )(x, g)


def _alibi_slopes(n_heads):
    return jnp.exp2(-8.0 * jnp.arange(1, n_heads + 1, dtype=F32) / n_heads)


def kernel(x_prompt, x_sample, cache_moba_k, cache_moba_v, page_table, state_win1, state_win2, state_win3,
           state_conv, c_prompt, c_sample, ada_w, ada_b, norm1_g, norm2_g, final_g, a_w_qkv, a_w_o,
           b_w_qkv, b_w_o, ffn_w_in, ffn_conv_w, ffn_conv_b, ffn_w_down):
    depth = ada_w.shape[0]
    bp, tp, d = x_prompt.shape
    bs, ts, _ = x_sample.shape
    n_heads_a = d // HEAD_DIM_A
    dff = ffn_w_down.shape[1]
    states = (state_win1, state_win2, state_win3)

    slopes_a = _alibi_slopes(n_heads_a)
    slopes_b = _alibi_slopes(N_GROUPS_B * HEADS_PER_GROUP_B)
    col = jnp.arange(LANES)
    live = col < ts * n_heads_a
    slope_cols = jnp.where(live, slopes_a[col % n_heads_a], 0.0).reshape(1, LANES).astype(F32)
    tq_cols = jnp.where(live, col // n_heads_a, 0).reshape(1, LANES).astype(jnp.int32)

    mods = _ada_all(jnp.concatenate([c_prompt, c_sample], axis=0), ada_w, ada_b)
    mods = mods.reshape(depth, bp + bs, 6, d)

    pool, page = cache_moba_k.shape[1:3]
    cache_k = cache_moba_k.reshape(cache_moba_k.shape[0], pool, page, d)
    cache_v = cache_moba_v.reshape(cache_moba_v.shape[0], pool, page, d)

    xp = x_prompt
    xs = x_sample.reshape(1, bs * ts, d)
    ka_p, va_p, ka_s, va_s = [], [], [], []
    win_p = [[] for _ in range(N_GROUPS_B)]
    win_s = [[] for _ in range(N_GROUPS_B)]
    conv_p, conv_s = [], []
    for i in range(depth):
        mp = [mods[i, :bp, k].reshape(bp, 1, d) for k in range(6)]
        ms = [jnp.repeat(mods[i, bp:, k], ts, axis=0).reshape(1, bs * ts, d) for k in range(6)]
        g1 = norm1_g[i].reshape(1, d)
        g2 = norm2_g[i].reshape(1, d)
        j = i // 2
        if i % 2 == 0:
            w_qkv = a_w_qkv[j].astype(BF16)
            w_o = a_w_o[j].astype(BF16)
            qkv_p = _nm_matmul(xp, mp[0], mp[1], g1, w_qkv)
            qkv_s = _nm_matmul(xs, ms[0], ms[1], g1, w_qkv).reshape(bs, ts, 3 * d)
            o_p = _moba_prompt_attn(qkv_p, slopes_a)
            o_s = _moba_sample_attn(qkv_s, cache_k, cache_v, j, page_table, slope_cols, tq_cols)
            xp = _oproj([o_p], w_o, xp, mp[2])
            xs = _oproj([o_s.reshape(1, bs * ts, d)], w_o, xs, ms[2])
            ka_p.append(qkv_p[:, :, d:2 * d].reshape(bp, tp, n_heads_a, HEAD_DIM_A))
            va_p.append(qkv_p[:, :, 2 * d:].reshape(bp, tp, n_heads_a, HEAD_DIM_A))
            ka_s.append(qkv_s[:, :, d:2 * d].reshape(bs, ts, n_heads_a, HEAD_DIM_A))
            va_s.append(qkv_s[:, :, 2 * d:].reshape(bs, ts, n_heads_a, HEAD_DIM_A))
        else:
            w_qkv = b_w_qkv[j].astype(BF16)
            w_o = b_w_o[j].astype(BF16)
            qkv_p = _nm_matmul(xp, mp[0], mp[1], g1, w_qkv)
            qkv_s = _nm_matmul(xs, ms[0], ms[1], g1, w_qkv).reshape(bs, ts, -1)
            parts = [_dilated_prompt_group(qkv_p, slopes_b, g) for g in range(N_GROUPS_B)]
            xp = _oproj([p[0] for p in parts] + [p[1] for p in parts], w_o, xp, mp[2])
            o_s = _dilated_sample_attn(qkv_s, states, j, slopes_b)
            xs = _oproj([o_s.reshape(1, bs * ts, D_GROUP_B)], w_o, xs, ms[2])
            kv_p = qkv_p.reshape(bp, tp, 3, N_GROUPS_B, HEADS_PER_GROUP_B, HEAD_DIM_B)[:, :, 1:]
            kv_s = qkv_s.reshape(bs, ts, 3, N_GROUPS_B, HEADS_PER_GROUP_B, HEAD_DIM_B)[:, :, 1:]
            for g in range(N_GROUPS_B):
                win_p[g].append(kv_p[:, -min(WINDOWS[g], tp):, :, g])
                wb = states[g].shape[2]
                win_s[g].append(jnp.concatenate([states[g][j], kv_s[:, :, :, g]], axis=1)[:, -wb:])
        w_in = ffn_w_in[i].astype(BF16)
        w_down = ffn_w_down[i].astype(BF16)
        cw = ffn_conv_w[i]
        cb = ffn_conv_b[i].reshape(1, dff)
        xp, a_tail = _conv_ffn(xp, mp[3], mp[4], mp[5], g2, w_in, cw, cb, w_down, tp)
        conv_p.append(a_tail[:, -(CONV_W - 1):])
        st = state_conv[i]
        zero = jnp.zeros((bs, 1, dff), F32)
        p1 = jnp.concatenate([st[:, 1:2], jnp.tile(zero, (1, ts - 1, 1))], axis=1).reshape(1, bs * ts, dff)
        p2 = jnp.concatenate([st[:, 0:1], st[:, 1:2], jnp.tile(zero, (1, ts - 2, 1))], axis=1).reshape(1, bs * ts, dff)
        xs, a_s = _conv_ffn(xs, ms[3], ms[4], ms[5], g2, w_in, cw, cb, w_down, ts, p1, p2)
        a_ext = jnp.concatenate([st, a_s.reshape(bs, ts, dff)], axis=1)
        conv_s.append(a_ext[:, -(CONV_W - 1):])
    y_prompt = _final_norm(xp, final_g.reshape(1, d))
    y_sample = _final_norm(xs, final_g.reshape(1, d)).reshape(bs, ts, d)
    return (y_prompt, y_sample, jnp.stack(ka_p), jnp.stack(va_p), jnp.stack(ka_s), jnp.stack(va_s),
            jnp.stack(win_p[0]), jnp.stack(win_p[1]), jnp.stack(win_p[2]),
            jnp.stack(win_s[0]), jnp.stack(win_s[1]), jnp.stack(win_s[2]),
            jnp.stack(conv_p), jnp.stack(conv_s))
```

```python
import functools

import jax
import jax.numpy as jnp
from jax import lax
from jax.experimental import pallas as pl
from jax.experimental.pallas import tpu as pltpu

F32 = jnp.float32
BF16 = jnp.bfloat16

HEAD_DIM_A = 64
MOBA_BLOCK = 256
MOBA_TOPK = 3
HEAD_DIM_B = 128
N_GROUPS_B = 3
HEADS_PER_GROUP_B = 4
D_GROUP_B = HEADS_PER_GROUP_B * HEAD_DIM_B
WINDOWS = (128, 512, 2048)
DILATIONS = (1, 4, 16)
CONV_W = 3
RMS_EPS = 1e-6
NEG_INF = -1e30
POS_BIG = 1e30

LANES = 128
SUBLANES = 8
VMEM_LIMIT = 56 * 1024 * 1024

MOBA_HEADS_PER_STEP = 8
MOBA_BLOCKS_PER_ITER = 2
MOBA_PAGES_PER_STEP = 4

_NT = (((1,), (1,)), ((), ()))


def _cparams(*sem):
    return pltpu.CompilerParams(dimension_semantics=sem, vmem_limit_bytes=VMEM_LIMIT)


def _row_tile(t, cap=512):
    tm = min(t, cap)
    assert t % tm == 0
    return tm


def _norm_mod(x, g, sh, sc):
    ms = jnp.mean(x * x, axis=-1, keepdims=True)
    y = x * lax.rsqrt(ms + RMS_EPS) * g
    return y * (1.0 + sc) + sh


def _ada_body(c_ref, w_ref, b_ref, o_ref):
    c = c_ref[...]
    s = (c * jax.nn.sigmoid(c)).astype(BF16)
    o_ref[0] = jnp.dot(s, w_ref[0].astype(BF16), preferred_element_type=F32) + b_ref[0]


def _ada_all(c_all, ada_w, ada_b):
    depth, d, n6 = ada_w.shape
    nb = c_all.shape[0]
    tn = n6 // 4
    return pl.pallas_call(
        _ada_body,
        grid=(depth, n6 // tn),
        in_specs=[pl.BlockSpec((nb, d), lambda l, n: (0, 0)),
                  pl.BlockSpec((1, d, tn), lambda l, n: (l, 0, n)),
                  pl.BlockSpec((1, 1, tn), lambda l, n: (l, 0, n))],
        out_specs=pl.BlockSpec((1, nb, tn), lambda l, n: (l, 0, n)),
        out_shape=jax.ShapeDtypeStruct((depth, nb, n6), F32),
        compiler_params=_cparams("parallel", "parallel"),
        name="ada_table",
    )(c_all, ada_w, ada_b.reshape(depth, 1, n6))


def _nm_matmul_body(x_ref, sh_ref, sc_ref, g_ref, w_ref, o_ref, h_s):
    @pl.when(pl.program_id(2) == 0)
    def _():
        h_s[...] = _norm_mod(x_ref[0], g_ref[...], sh_ref[0], sc_ref[0]).astype(BF16)

    o_ref[0] = jnp.dot(h_s[...], w_ref[...], preferred_element_type=F32)


def _nm_matmul(x, sh, sc, g, w, n_tiles=3):
    b, t, d = x.shape
    n = w.shape[1]
    tm = _row_tile(t)
    tmod = sh.shape[1]
    tmb = tm if tmod == t else 1
    mod_map = (lambda bi, mi, ni: (bi, mi, 0)) if tmod == t else (lambda bi, mi, ni: (bi, 0, 0))
    tn = n // n_tiles
    return pl.pallas_call(
        _nm_matmul_body,
        grid=(b, t // tm, n_tiles),
        in_specs=[pl.BlockSpec((1, tm, d), lambda bi, mi, ni: (bi, mi, 0)),
                  pl.BlockSpec((1, tmb, d), mod_map),
                  pl.BlockSpec((1, tmb, d), mod_map),
                  pl.BlockSpec((1, d), lambda bi, mi, ni: (0, 0)),
                  pl.BlockSpec((d, tn), lambda bi, mi, ni: (0, ni))],
        out_specs=pl.BlockSpec((1, tm, tn), lambda bi, mi, ni: (bi, mi, ni)),
        out_shape=jax.ShapeDtypeStruct((b, t, n), F32),
        scratch_shapes=[pltpu.VMEM((tm, d), BF16)],
        compiler_params=_cparams("parallel", "parallel", "arbitrary"),
        name="norm_mod_matmul",
    )(x, sh, sc, g, w)


def _moba_qkv_body(x_ref, sh_ref, sc_ref, g_ref, wq_ref, wk_ref, wkt_ref, wvt_ref,
                   q_ref, kb_ref, kt_ref, vt_ref, vtb_ref, km_ref):
    tm = x_ref.shape[1]
    per_tile = tm // MOBA_BLOCK
    h = _norm_mod(x_ref[0], g_ref[...], sh_ref[0], sc_ref[0]).astype(BF16)
    q_ref[0] = jnp.dot(h, wq_ref[...], preferred_element_type=F32)
    k = jnp.dot(h, wk_ref[...], preferred_element_type=F32)
    kb_ref[0] = k.astype(BF16)
    first = pl.program_id(1) * per_tile
    for j in range(per_tile):
        km_ref[0, pl.ds(first + j, 1), :] = (
            jnp.sum(k[j * MOBA_BLOCK:(j + 1) * MOBA_BLOCK], axis=0, keepdims=True) * (1.0 / MOBA_BLOCK))
    kt_ref[0] = lax.dot_general(wkt_ref[...], h, _NT, preferred_element_type=F32)
    vt = lax.dot_general(wvt_ref[...], h, _NT, preferred_element_type=F32)
    vt_ref[0] = vt
    for j in range(per_tile):
        vtb_ref[0, j] = vt[:, j * MOBA_BLOCK:(j + 1) * MOBA_BLOCK].astype(BF16)


def _moba_qkv_prompt(x, sh, sc, g, w_qkv):
    b, t, d = x.shape
    tm = _row_tile(t)
    assert tm % MOBA_BLOCK == 0
    nb = t // MOBA_BLOCK
    wq = w_qkv[:, :d].astype(BF16)
    wk = w_qkv[:, d:2 * d].astype(BF16)
    wkt = w_qkv[:, d:2 * d].T.astype(BF16)
    wvt = w_qkv[:, 2 * d:].T.astype(BF16)
    row = lambda bi, mi: (bi, mi, 0)
    col = lambda bi, mi: (bi, 0, mi)
    mod = lambda bi, mi: (bi, 0, 0)
    const = lambda bi, mi: (0, 0)
    return pl.pallas_call(
        _moba_qkv_body,
        grid=(b, t // tm),
        in_specs=[pl.BlockSpec((1, tm, d), row),
                  pl.BlockSpec((1, 1, d), mod),
                  pl.BlockSpec((1, 1, d), mod),
                  pl.BlockSpec((1, d), const)] + [pl.BlockSpec((d, d), const)] * 4,
        out_specs=[pl.BlockSpec((1, tm, d), row),
                   pl.BlockSpec((1, tm, d), row),
                   pl.BlockSpec((1, d, tm), col),
                   pl.BlockSpec((1, d, tm), col),
                   pl.BlockSpec((1, tm // MOBA_BLOCK, d, MOBA_BLOCK), lambda bi, mi: (bi, mi, 0, 0)),
                   pl.BlockSpec((1, nb, d), mod)],
        out_shape=[jax.ShapeDtypeStruct((b, t, d), F32),
                   jax.ShapeDtypeStruct((b, t, d), BF16),
                   jax.ShapeDtypeStruct((b, d, t), F32),
                   jax.ShapeDtypeStruct((b, d, t), F32),
                   jax.ShapeDtypeStruct((b, nb, d, MOBA_BLOCK), BF16),
                   jax.ShapeDtypeStruct((b, nb, d), F32)],
        compiler_params=_cparams("parallel", "arbitrary"),
        name="moba_qkv_prompt",
    )(x, sh, sc, g, wq, wk, wkt, wvt)


def _topk_rank_select(gate, blk, n_valid, n_rows):
    gate = jnp.where(blk < n_valid, gate, NEG_INF)
    rank = jnp.zeros(gate.shape, F32)
    for m in range(n_rows):
        gm = gate[m:m + 1, :]
        beats = (gm > gate) | ((gm == gate) & (blk > m))
        rank = rank + jnp.where(beats, 1.0, 0.0)
    n_sel = jnp.minimum(n_valid, MOBA_TOPK).astype(F32)
    return jnp.where((rank < n_sel) & (blk < n_valid), 1.0, 0.0)


def _moba_prompt_body(slopes_ref, q_ref, kb_ref, vtb_ref, km_ref, o_ref,
                      d0_s, sel_s, qb_s, m_s, l_s, acc_s, s_s, *, nb, hps, per_iter):
    blk_sz, hd = MOBA_BLOCK, HEAD_DIM_A
    head0 = pl.program_id(1) * hps
    i = pl.program_id(2)

    @pl.when(i == 0)
    def _init():
        rk = lax.broadcasted_iota(jnp.int32, (blk_sz, blk_sz), 0)
        rq = lax.broadcasted_iota(jnp.int32, (blk_sz, blk_sz), 1)
        rel = (rk - rq).astype(F32)
        for h in range(hps):
            d0 = slopes_ref[head0 + h] * rel
            d0_s[h, 0] = d0
            d0_s[h, 1] = jnp.where(rk <= rq, d0, NEG_INF)

    lane = lax.broadcasted_iota(jnp.int32, (1, LANES), 1)
    blk = lax.broadcasted_iota(jnp.int32, (nb, blk_sz), 0)
    for h in range(hps):
        pair, hh = divmod(h, LANES // hd)
        cs = slice(pair * LANES, (pair + 1) * LANES)
        qh = jnp.where((lane >= hh * hd) & (lane < (hh + 1) * hd), q_ref[0, :, cs] * (hd ** -0.5), 0.0)
        gate = lax.dot_general(km_ref[0, :, cs], qh, _NT, precision=lax.Precision.HIGHEST,
                               preferred_element_type=F32)
        sel_s[h] = jnp.where(blk == i, 1.0, _topk_rank_select(gate, blk, i, nb))
        qb_s[h] = qh.astype(BF16)
        m_s[h] = jnp.full((1, blk_sz), NEG_INF, F32)
        l_s[h] = jnp.zeros((1, blk_sz), F32)
        acc_s[h] = jnp.zeros((hd, blk_sz), F32)

    def key_blocks(it, carry):
        blocks = []
        for u in range(per_iter):
            n = it * per_iter + u
            n_mem = jnp.minimum(n, nb - 1)
            rows = pl.ds(pl.multiple_of(n_mem * blk_sz, blk_sz), blk_sz)
            for h in range(hps):
                pair = h // (LANES // hd)
                kblk = kb_ref[0, rows, pair * LANES:(pair + 1) * LANES]
                s_s[u, h] = lax.dot_general(kblk, qb_s[h], _NT, preferred_element_type=F32)
            blocks.append((n, n_mem))
        for u, (n, n_mem) in enumerate(blocks):
            off = ((i - n) * blk_sz).astype(F32)
            own = (n == i).astype(jnp.int32)
            for h in range(hps):
                s0 = s_s[u, h] + d0_s[h, own]
                c_n = -slopes_ref[head0 + h] * off
                selrow = (sel_s[h, pl.ds(n_mem, 1), :] > 0.5) & (n <= i)
                m_old = m_s[h]
                m_new = jnp.maximum(m_old, jnp.where(selrow, jnp.max(s0, axis=0, keepdims=True) + c_n, NEG_INF))
                alpha = jnp.exp(m_old - m_new)
                p = jnp.exp(s0 - jnp.where(selrow, m_new - c_n, POS_BIG))
                l_s[h] = alpha * l_s[h] + jnp.sum(p, axis=0, keepdims=True)
                vt = vtb_ref[0, n_mem, h * hd:(h + 1) * hd, :]
                acc_s[h] = alpha * acc_s[h] + jnp.dot(vt, p.astype(BF16), preferred_element_type=F32)
                m_s[h] = m_new
        return carry

    lax.fori_loop(0, (i + per_iter) // per_iter, key_blocks, 0)
    per_pair = LANES // hd
    for pair in range(hps // per_pair):
        outs = [acc_s[pair * per_pair + hh] / l_s[pair * per_pair + hh] for hh in range(per_pair)]
        o_ref[0, :, pair * LANES:(pair + 1) * LANES] = jnp.concatenate(outs, axis=0).T.astype(o_ref.dtype)


def _moba_prompt_attn(q, kb, vtb, km, slopes):
    b, t, d = q.shape
    nb = t // MOBA_BLOCK
    hps = MOBA_HEADS_PER_STEP
    w = hps * HEAD_DIM_A
    assert t % MOBA_BLOCK == 0 and d % w == 0 and w % LANES == 0
    return pl.pallas_call(
        functools.partial(_moba_prompt_body, nb=nb, hps=hps, per_iter=MOBA_BLOCKS_PER_ITER),
        grid=(b, d // w, nb),
        in_specs=[pl.BlockSpec(memory_space=pltpu.SMEM),
                  pl.BlockSpec((1, MOBA_BLOCK, w), lambda bi, hg, i: (bi, i, hg)),
                  pl.BlockSpec((1, t, w), lambda bi, hg, i: (bi, 0, hg)),
                  pl.BlockSpec((1, nb, w, MOBA_BLOCK), lambda bi, hg, i: (bi, 0, hg, 0)),
                  pl.BlockSpec((1, nb, w), lambda bi, hg, i: (bi, 0, hg))],
        out_specs=pl.BlockSpec((1, MOBA_BLOCK, w), lambda bi, hg, i: (bi, i, hg)),
        out_shape=jax.ShapeDtypeStruct((b, t, d), BF16),
        scratch_shapes=[pltpu.VMEM((hps, 2, MOBA_BLOCK, MOBA_BLOCK), F32),
                        pltpu.VMEM((hps, nb, MOBA_BLOCK), F32),
                        pltpu.VMEM((hps, MOBA_BLOCK, LANES), BF16),
                        pltpu.VMEM((hps, 1, MOBA_BLOCK), F32),
                        pltpu.VMEM((hps, 1, MOBA_BLOCK), F32),
                        pltpu.VMEM((hps, HEAD_DIM_A, MOBA_BLOCK), F32),
                        pltpu.VMEM((MOBA_BLOCKS_PER_ITER, hps, MOBA_BLOCK, MOBA_BLOCK), F32)],
        compiler_params=_cparams("parallel", "parallel", "arbitrary"),
        name="moba_prompt_attn",
    )(slopes, q, kb, vtb, km)


def _head_block_diag(q, n_heads, hd):
    d = n_heads * hd
    n_t = LANES // n_heads
    row_h = lax.broadcasted_iota(jnp.int32, (n_heads, d), 0)
    col_h = lax.broadcasted_iota(jnp.int32, (n_heads, d), 1) // hd
    pieces = []
    for t in range(n_t):
        if t < q.shape[0]:
            pieces.append(jnp.where(row_h == col_h, jnp.broadcast_to(q[t:t + 1, :], (n_heads, d)), 0.0))
        else:
            pieces.append(jnp.zeros((n_heads, d), F32))
    return jnp.concatenate(pieces, axis=0)


def _moba_s_scores_body(pt_ref, qt_ref, *refs, n_heads, t_dec, pages):
    k_refs, s_ref, qc_s = refs[:pages], refs[pages], refs[pages + 1]
    hd = HEAD_DIM_A
    page = k_refs[0].shape[-1]
    per_blk = MOBA_BLOCK // page

    @pl.when(pl.program_id(1) == 0)
    def _():
        for t in range(t_dec):
            for h in range(n_heads):
                col = qt_ref[0, h * hd:(h + 1) * hd, t:t + 1] * (hd ** -0.5)
                qc_s[t * n_heads + h] = jnp.broadcast_to(col, (hd, page))

    for p, k_ref in enumerate(k_refs):
        lanes = slice((p % per_blk) * page, (p % per_blk + 1) * page)
        for h in range(n_heads):
            kt = k_ref[0, 0, h]
            for t in range(t_dec):
                r = t * n_heads + h
                s_ref[0, p // per_blk, r:r + 1, lanes] = jnp.sum(kt * qc_s[r], axis=0, keepdims=True)


def _moba_s_softmax_body(s_ref, q_ref, kn_ref, sl_ref, tq_ref, p_ref, po_ref, l_ref,
                         *, n_heads, n_blocks, t_dec, past_len):
    hd = HEAD_DIM_A
    rows = t_dec * n_heads
    blk = MOBA_BLOCK
    lane = lax.broadcasted_iota(jnp.int32, (rows, LANES), 1)
    gate = jnp.full((rows, LANES), NEG_INF, F32)
    for n in range(n_blocks):
        gate = jnp.where(lane == n, jnp.sum(s_ref[0, n], axis=1, keepdims=True), gate)
    rank = jnp.zeros((rows, LANES), F32)
    for m in range(n_blocks):
        gm = gate[:, m:m + 1]
        beats = (gm > gate) | ((gm == gate) & (lane > m))
        rank = rank + jnp.where(beats, 1.0, 0.0)
    sel = (rank < min(MOBA_TOPK, n_blocks)) & (lane < n_blocks)

    sl = sl_ref[...]
    tq = tq_ref[...]
    tok = lax.broadcasted_iota(jnp.int32, (rows, blk), 1).astype(F32)
    dist0 = past_len + tq - tok
    sl1, tq1 = sl[:, 0:1], tq[:, 0:1]

    qbd = _head_block_diag(q_ref[0] * (hd ** -0.5), n_heads, hd)[:rows]
    kn = kn_ref[0]
    own = []
    for tk in range(t_dec):
        s = jnp.sum(qbd * kn[tk:tk + 1, :], axis=1, keepdims=True)
        dd = tq1 - tk
        own.append(jnp.where(dd >= 0, s - sl1 * dd, NEG_INF))
    m = own[0]
    for s in own[1:]:
        m = jnp.maximum(m, s)
    for n in range(n_blocks):
        sb = s_ref[0, n] - sl * (dist0 - n * blk)
        m = jnp.maximum(m, jnp.where(sel[:, n:n + 1], jnp.max(sb, axis=1, keepdims=True), NEG_INF))
    l = jnp.zeros((rows, 1), F32)
    for n in range(n_blocks):
        sb = s_ref[0, n] - sl * (dist0 - n * blk)
        p = jnp.exp(sb - jnp.where(sel[:, n:n + 1], m, POS_BIG))
        l = l + jnp.sum(p, axis=1, keepdims=True)
        p_ref[0, n, 0:rows, :] = p.astype(BF16)
        p_ref[0, n, rows:, :] = jnp.zeros((p_ref.shape[2] - rows, blk), BF16)
    po = jnp.zeros((rows, LANES), F32)
    for tk in range(t_dec):
        pk = jnp.exp(own[tk] - m)
        l = l + pk
        po = jnp.where(lane == tk, pk, po)
    po_ref[0] = po
    l_ref[0] = jnp.broadcast_to(l, (rows, LANES))


def _moba_s_pv_body(pt_ref, p_ref, po_ref, l_ref, vn_ref, *refs, n_heads, t_dec, pages):
    v_refs, o_ref, acc_s = refs[:pages], refs[pages], refs[pages + 1]
    hd = HEAD_DIM_A
    n = pl.program_id(1)
    page = v_refs[0].shape[-1]
    per_blk = MOBA_BLOCK // page
    d = n_heads * hd
    rows = t_dec * n_heads

    @pl.when(n == 0)
    def _():
        acc_s[...] = jnp.zeros(acc_s.shape, F32)

    tot = None
    for p, v_ref in enumerate(v_refs):
        vt = v_ref[0, 0].reshape(d, page).astype(BF16)
        pp = p_ref[0, p // per_blk][:, (p % per_blk) * page:(p % per_blk + 1) * page]
        y = lax.dot_general(vt, pp, _NT, preferred_element_type=F32)
        tot = y if tot is None else tot + y
    acc_s[...] += tot

    @pl.when(n == pl.num_programs(1) - 1)
    def _():
        acc = acc_s[...].T[:rows]
        po = po_ref[0]
        vn = vn_ref[0]
        for tk in range(t_dec):
            acc = acc + po[:, tk:tk + 1] * vn[tk:tk + 1, :]
        row_h = lax.broadcasted_iota(jnp.int32, (rows, d), 0) % n_heads
        col_h = lax.broadcasted_iota(jnp.int32, (rows, d), 1) // hd
        acc = jnp.where(row_h == col_h, acc / l_ref[0][:, 0:1], 0.0)
        for t in range(t_dec):
            o_ref[0, t:t + 1, :] = jnp.sum(acc[t * n_heads:(t + 1) * n_heads, :], axis=0, keepdims=True)


def _moba_sample_attn(qkv_s, cache_kt, cache_vt, layer, page_table, slope_rows, tq_rows):
    bs, t_dec, d3 = qkv_s.shape
    d = d3 // 3
    n_heads, hd, page = cache_kt.shape[2:]
    n_pages = page_table.shape[1]
    past_len = n_pages * page
    pages = MOBA_PAGES_PER_STEP
    rows = t_dec * n_heads
    assert hd == HEAD_DIM_A and n_heads * hd == d
    assert MOBA_BLOCK % page == 0 and past_len % MOBA_BLOCK == 0 and page == LANES
    assert rows % SUBLANES == 0 and rows <= LANES and t_dec <= SUBLANES
    n_blocks = past_len // MOBA_BLOCK
    blocks_per_step = pages * page // MOBA_BLOCK
    assert MOBA_TOPK <= n_blocks <= LANES and n_pages % pages == 0 and (pages * page) % MOBA_BLOCK == 0
    n_steps = n_pages // pages
    pad = SUBLANES - t_dec
    q_s = qkv_s[:, :, :d]
    q8 = jnp.pad(q_s, ((0, 0), (0, pad), (0, 0)))
    k8 = jnp.pad(qkv_s[:, :, d:2 * d], ((0, 0), (0, pad), (0, 0)))
    v8 = jnp.pad(qkv_s[:, :, 2 * d:], ((0, 0), (0, pad), (0, 0)))
    q_t = jnp.transpose(q_s, (0, 2, 1))

    def page_spec(p):
        return pl.BlockSpec((1, 1, n_heads, hd, page),
                            lambda bi, n, pt: (layer, pt[bi, pages * n + p], 0, 0, 0))

    small = lambda bi, n, pt: (bi, 0, 0)
    s_all = pl.pallas_call(
        functools.partial(_moba_s_scores_body, n_heads=n_heads, t_dec=t_dec, pages=pages),
        grid_spec=pltpu.PrefetchScalarGridSpec(
            num_scalar_prefetch=1,
            grid=(bs, n_steps),
            in_specs=[pl.BlockSpec((1, d, t_dec), small)] + [page_spec(p) for p in range(pages)],
            out_specs=pl.BlockSpec((1, blocks_per_step, rows, MOBA_BLOCK), lambda bi, n, pt: (bi, n, 0, 0)),
            scratch_shapes=[pltpu.VMEM((rows, hd, page), F32)]),
        out_shape=jax.ShapeDtypeStruct((bs, n_blocks, rows, MOBA_BLOCK), F32),
        compiler_params=_cparams("parallel", "arbitrary"),
        name="moba_sample_scores",
    )(page_table, q_t, *([cache_kt] * pages))

    per_b = lambda bi: (bi, 0, 0)
    const = lambda bi: (0, 0)
    p_all, p_own, l_sum = pl.pallas_call(
        functools.partial(_moba_s_softmax_body, n_heads=n_heads, n_blocks=n_blocks, t_dec=t_dec,
                          past_len=past_len),
        grid=(bs,),
        in_specs=[pl.BlockSpec((1, n_blocks, rows, MOBA_BLOCK), lambda bi: (bi, 0, 0, 0)),
                  pl.BlockSpec((1, SUBLANES, d), per_b),
                  pl.BlockSpec((1, SUBLANES, d), per_b),
                  pl.BlockSpec((rows, MOBA_BLOCK), const),
                  pl.BlockSpec((rows, MOBA_BLOCK), const)],
        out_specs=[pl.BlockSpec((1, n_blocks, LANES, MOBA_BLOCK), lambda bi: (bi, 0, 0, 0)),
                   pl.BlockSpec((1, rows, LANES), per_b),
                   pl.BlockSpec((1, rows, LANES), per_b)],
        out_shape=[jax.ShapeDtypeStruct((bs, n_blocks, LANES, MOBA_BLOCK), BF16),
                   jax.ShapeDtypeStruct((bs, rows, LANES), F32),
                   jax.ShapeDtypeStruct((bs, rows, LANES), F32)],
        compiler_params=_cparams("parallel"),
        name="moba_sample_softmax",
    )(s_all, q8, k8, slope_rows, tq_rows)

    return pl.pallas_call(
        functools.partial(_moba_s_pv_body, n_heads=n_heads, t_dec=t_dec, pages=pages),
        grid_spec=pltpu.PrefetchScalarGridSpec(
            num_scalar_prefetch=1,
            grid=(bs, n_steps),
            in_specs=[pl.BlockSpec((1, blocks_per_step, LANES, MOBA_BLOCK), lambda bi, n, pt: (bi, n, 0, 0)),
                      pl.BlockSpec((1, rows, LANES), small),
                      pl.BlockSpec((1, rows, LANES), small),
                      pl.BlockSpec((1, SUBLANES, d), small)] + [page_spec(p) for p in range(pages)],
            out_specs=pl.BlockSpec((1, t_dec, d), small),
            scratch_shapes=[pltpu.VMEM((d, LANES), F32)]),
        out_shape=jax.ShapeDtypeStruct((bs, t_dec, d), F32),
        compiler_params=_cparams("parallel", "arbitrary"),
        name="moba_sample_pv",
    )(page_table, p_all, p_own, l_sum, v8, *([cache_vt] * pages))


def _dilated_prompt_body(slopes_ref, q_ref, kp_ref, kc_ref, vp_ref, vc_ref, o_ref, lse_ref, bias_s,
                         *, group, dilation, n_keys):
    tq = q_ref.shape[1]
    hd = HEAD_DIM_B
    first = (pl.program_id(0) == 0) & (pl.program_id(1) == 0) & (pl.program_id(2) == 0)

    @pl.when(first)
    def _():
        iq = lax.broadcasted_iota(jnp.int32, (tq, 2 * tq), 0)
        jk = lax.broadcasted_iota(jnp.int32, (tq, 2 * tq), 1)
        steps = iq + tq - jk
        ok = (steps >= 0) & (steps < n_keys)
        dist = (steps * dilation).astype(F32)
        for h in range(HEADS_PER_GROUP_B):
            b = -slopes_ref[group * HEADS_PER_GROUP_B + h] * dist
            bias_s[0, h] = jnp.where(ok & (jk >= tq), b, NEG_INF)
            bias_s[1, h] = jnp.where(ok, b, NEG_INF)

    var = jnp.minimum(pl.program_id(2), 1)
    scale = hd ** -0.5
    def scores(h):
        cs = slice(h * hd, (h + 1) * hd)
        qh = q_ref[0, :, cs].astype(BF16)
        return (lax.dot_general(qh, kp_ref[0, :, cs].astype(BF16), _NT, preferred_element_type=F32),
                lax.dot_general(qh, kc_ref[0, :, cs].astype(BF16), _NT, preferred_element_type=F32))

    s_next = scores(0)
    for h in range(HEADS_PER_GROUP_B):
        cs = slice(h * hd, (h + 1) * hd)
        bias = bias_s[var, h]
        s_p = s_next[0] * scale + bias[:, :tq]
        s_c = s_next[1] * scale + bias[:, tq:]
        if h + 1 < HEADS_PER_GROUP_B:
            s_next = scores(h + 1)
        m = jnp.maximum(jnp.max(s_p, axis=1, keepdims=True), jnp.max(s_c, axis=1, keepdims=True))
        p_p = jnp.exp(s_p - m)
        p_c = jnp.exp(s_c - m)
        l = jnp.sum(p_p, axis=1, keepdims=True) + jnp.sum(p_c, axis=1, keepdims=True)
        o = (jnp.dot(p_p.astype(BF16), vp_ref[0, :, cs].astype(BF16), preferred_element_type=F32)
             + jnp.dot(p_c.astype(BF16), vc_ref[0, :, cs].astype(BF16), preferred_element_type=F32))
        o_ref[0, :, cs] = o / l
        lse_ref[0, :, cs] = jnp.broadcast_to(m + jnp.log(l), (tq, hd))


def _dilated_prompt_group(qkv, slopes, group):
    b, t, n_all = qkv.shape
    dg = D_GROUP_B
    dil = DILATIONS[group]
    n_keys = WINDOWS[group] // dil + 1
    tq = n_keys - 1
    tr = t // dil
    assert t % dil == 0 and tr % tq == 0 and tq % LANES == 0
    cols = n_all // dg
    view = qkv.reshape(b, tr, dil * n_all)

    def sec(section, prev):
        def index(bi, r, ti):
            row = jnp.maximum(ti - 1, 0) if prev else ti
            return (bi, row, r * cols + section * N_GROUPS_B + group)
        return pl.BlockSpec((1, tq, dg), index)

    out_spec = pl.BlockSpec((1, tq, dg), lambda bi, r, ti: (bi, ti, r))
    o, lse = pl.pallas_call(
        functools.partial(_dilated_prompt_body, group=group, dilation=dil, n_keys=n_keys),
        grid=(b, dil, tr // tq),
        in_specs=[pl.BlockSpec(memory_space=pltpu.SMEM),
                  sec(0, False), sec(1, True), sec(1, False), sec(2, True), sec(2, False)],
        out_specs=[out_spec, out_spec],
        out_shape=[jax.ShapeDtypeStruct((b, tr, dil * dg), F32)] * 2,
        scratch_shapes=[pltpu.VMEM((2, HEADS_PER_GROUP_B, tq, 2 * tq), F32)],
        compiler_params=_cparams("arbitrary", "arbitrary", "arbitrary"),
        name=f"dilated_prompt_g{group}",
    )(slopes, view, view, view, view, view)
    return o.reshape(b, t, dg), lse.reshape(b, t, dg)


def _dilated_sample_body(slopes_ref, x_ref, b0_ref, b1_ref, b2_ref, o_ref, *, t_dec):
    hd = HEAD_DIM_B
    dg = D_GROUP_B
    scale = hd ** -0.5
    x = x_ref[0]
    rows = x.shape[0]
    tq = lax.broadcasted_iota(jnp.int32, (rows, LANES), 0)
    mcol = lax.broadcasted_iota(jnp.int32, (rows, LANES), 1)
    tq1 = lax.broadcasted_iota(jnp.int32, (rows, 1), 0)
    bufs = (b0_ref, b1_ref, b2_ref)
    for h in range(HEADS_PER_GROUP_B):
        outs, lses = [], []
        for g in range(N_GROUPS_B):
            dil = DILATIONS[g]
            n_back = WINDOWS[g] // dil
            slope = slopes_ref[g * HEADS_PER_GROUP_B + h]
            c0 = g * dg + h * hd
            qh = x[:, c0:c0 + hd]
            kn = x[:, N_GROUPS_B * dg + c0:N_GROUPS_B * dg + c0 + hd]
            vn = x[:, 2 * N_GROUPS_B * dg + c0:2 * N_GROUPS_B * dg + c0 + hd]
            qb = qh.astype(BF16)
            s_list, v_list = [], []
            for r in range(min(dil, t_dec)):
                base = r * 2 * dg + h * hd
                kr = bufs[g][0, :, base:base + hd]
                v_list.append(bufs[g][0, :, base + dg:base + dg + hd])
                num = dil * (n_back - mcol) + (tq - r)
                ok = (num >= 0) & (num <= dil * n_back) & ((num & (dil - 1)) == 0)
                s = lax.dot_general(qb, kr.astype(BF16), _NT, preferred_element_type=F32) * scale
                s_list.append(jnp.where(ok, s - slope * num.astype(F32), NEG_INF))
            s_new = []
            for tk in range(t_dec):
                num = tq1 - tk
                ok = (num >= 0) & (num <= dil * n_back) & ((num & (dil - 1)) == 0)
                s = jnp.sum(qh * kn[tk:tk + 1, :], axis=1, keepdims=True) * scale
                s_new.append(jnp.where(ok, s - slope * num.astype(F32), NEG_INF))
            m = s_new[0]
            for s in s_new[1:]:
                m = jnp.maximum(m, s)
            for s in s_list:
                m = jnp.maximum(m, jnp.max(s, axis=1, keepdims=True))
            l = jnp.zeros((rows, 1), F32)
            o = jnp.zeros((rows, hd), F32)
            for s, v in zip(s_list, v_list):
                p = jnp.exp(s - m)
                l = l + jnp.sum(p, axis=1, keepdims=True)
                o = o + jnp.dot(p.astype(BF16), v.astype(BF16), preferred_element_type=F32)
            for tk, s in enumerate(s_new):
                p = jnp.exp(s - m)
                l = l + p
                o = o + p * vn[tk:tk + 1, :]
            outs.append(o / l)
            lses.append(m + jnp.log(l))
        lm = jnp.maximum(jnp.maximum(lses[0], lses[1]), lses[2])
        es = [jnp.exp(ls - lm) for ls in lses]
        den = es[0] + es[1] + es[2]
        merged = (es[0] * outs[0] + es[1] * outs[1] + es[2] * outs[2]) / den
        o_ref[0, :, h * hd:(h + 1) * hd] = merged[:t_dec, :]


def _dilated_sample_attn(qkv_s, states, layer, slopes):
    bs, t_dec, n_all = qkv_s.shape
    dg = D_GROUP_B
    x8 = jnp.pad(qkv_s, ((0, 0), (0, SUBLANES - t_dec), (0, 0)))
    views, specs = [], []
    for g in range(N_GROUPS_B):
        dil = DILATIONS[g]
        n_l, _, wb = states[g].shape[:3]
        n_back = WINDOWS[g] // dil
        assert wb == WINDOWS[g] == n_back * dil and n_back == LANES
        assert dil & (dil - 1) == 0 and (dil == 1 or dil >= t_dec)
        n_res = min(dil, t_dec)
        used = states[g][layer].reshape(bs, n_back, dil, 2 * dg)[:, :, :n_res]
        views.append(used.reshape(bs, n_back, n_res * 2 * dg))
        specs.append(pl.BlockSpec((1, n_back, n_res * 2 * dg), lambda bi: (bi, 0, 0)))
    return pl.pallas_call(
        functools.partial(_dilated_sample_body, t_dec=t_dec),
        grid=(bs,),
        in_specs=[pl.BlockSpec(memory_space=pltpu.SMEM),
                  pl.BlockSpec((1, SUBLANES, n_all), lambda bi: (bi, 0, 0))] + specs,
        out_specs=pl.BlockSpec((1, t_dec, dg), lambda bi: (bi, 0, 0)),
        out_shape=jax.ShapeDtypeStruct((bs, t_dec, dg), F32),
        compiler_params=_cparams("parallel"),
        name="dilated_sample_attn",
    )(slopes, x8, *views)


def _win_update_body(*refs, n_groups, n_layers, t_dec):
    states = refs[:n_groups]
    news = refs[n_groups:2 * n_groups]
    outs = refs[2 * n_groups:3 * n_groups]
    sem = refs[3 * n_groups]
    copies = []
    for g in range(n_groups):
        wb = states[g].shape[2]
        for l in range(n_layers):
            copies.append(pltpu.make_async_copy(
                states[g].at[l, :, pl.ds(t_dec, wb - t_dec)], outs[g].at[l, :, pl.ds(0, wb - t_dec)],
                sem.at[len(copies)]))
            copies.append(pltpu.make_async_copy(
                news[g].at[l], outs[g].at[l, :, pl.ds(wb - t_dec, t_dec)], sem.at[len(copies)]))
    for c in copies:
        c.start()
    for c in copies:
        c.wait()


def _win_update(states, news):
    n_groups = len(states)
    n_layers = states[0].shape[0]
    t_dec = news[0].shape[2]
    assert all(s.shape[2] > t_dec for s in states)
    anyspec = pl.BlockSpec(memory_space=pl.ANY)
    return pl.pallas_call(
        functools.partial(_win_update_body, n_groups=n_groups, n_layers=n_layers, t_dec=t_dec),
        in_specs=[anyspec] * (2 * n_groups),
        out_specs=[anyspec] * n_groups,
        out_shape=[jax.ShapeDtypeStruct(s.shape, s.dtype) for s in states],
        scratch_shapes=[pltpu.SemaphoreType.DMA((2 * n_groups * n_layers,))],
        name="win_state_update",
    )(*states, *news)


def _oproj_body(o_ref, w_ref, x_ref, g_ref, out_ref):
    y = jnp.dot(o_ref[0].astype(BF16), w_ref[...], preferred_element_type=F32)
    out_ref[0] = x_ref[0] + g_ref[0] * y


def _oproj_merge_body(o0_ref, o1_ref, o2_ref, l0_ref, l1_ref, l2_ref, w_ref, x_ref, g_ref, out_ref):
    l0, l1, l2 = l0_ref[0], l1_ref[0], l2_ref[0]
    lm = jnp.maximum(jnp.maximum(l0, l1), l2)
    e0, e1, e2 = jnp.exp(l0 - lm), jnp.exp(l1 - lm), jnp.exp(l2 - lm)
    o = (e0 * o0_ref[0] + e1 * o1_ref[0] + e2 * o2_ref[0]) / (e0 + e1 + e2)
    y = jnp.dot(o.astype(BF16), w_ref[...], preferred_element_type=F32)
    out_ref[0] = x_ref[0] + g_ref[0] * y


def _oproj(o_list, w, x, gate):
    b, t, d = x.shape
    din = w.shape[0]
    tm = _row_tile(t)
    tmod = gate.shape[1]
    tmb = tm if tmod == t else 1
    gate_map = (lambda bi, mi: (bi, mi, 0)) if tmod == t else (lambda bi, mi: (bi, 0, 0))
    row = lambda bi, mi: (bi, mi, 0)
    body = _oproj_body if len(o_list) == 1 else _oproj_merge_body
    return pl.pallas_call(
        body,
        grid=(b, t // tm),
        in_specs=[pl.BlockSpec((1, tm, din), row)] * len(o_list) + [
            pl.BlockSpec((din, d), lambda bi, mi: (0, 0)),
            pl.BlockSpec((1, tm, d), row),
            pl.BlockSpec((1, tmb, d), gate_map)],
        out_specs=pl.BlockSpec((1, tm, d), row),
        out_shape=jax.ShapeDtypeStruct((b, t, d), F32),
        compiler_params=_cparams("parallel", "parallel"),
        name="out_proj_residual",
    )(*o_list, w, x, gate)


def _conv_ffn_body(x_ref, sh_ref, sc_ref, gt_ref, g_ref, win_ref, cw_ref, cb_ref, wd_ref, p1_ref, p2_ref,
                   out_ref, a_ref, z_s, carry_s, *, chunk, seq_len, carry_rows):
    tm = x_ref.shape[1]
    dff = wd_ref.shape[0]
    x = x_ref[0]
    h = _norm_mod(x, g_ref[...], sh_ref[0], sc_ref[0]).astype(BF16)
    row = lax.broadcasted_iota(jnp.int32, (tm, chunk), 0)
    t_in_seq = row % seq_len
    if carry_rows:
        @pl.when(pl.program_id(1) == 0)
        def _():
            carry_s[...] = jnp.zeros(carry_s.shape, F32)

    for c in range(dff // chunk):
        cs = slice(c * chunk, (c + 1) * chunk)
        a = jnp.dot(h, win_ref[:, cs], preferred_element_type=F32)
        bgate = jnp.dot(h, win_ref[:, dff + c * chunk:dff + (c + 1) * chunk], preferred_element_type=F32)
        a1 = jnp.where(t_in_seq >= 1, pltpu.roll(a, 1, axis=0), 0.0)
        a2 = jnp.where(t_in_seq >= 2, pltpu.roll(a, 2, axis=0), 0.0)
        if carry_rows:
            prev = carry_s[:, cs]
            a1 = jnp.where(row == 0, prev[SUBLANES - 1:SUBLANES, :], a1)
            a2 = jnp.where(row == 0, prev[SUBLANES - 2:SUBLANES - 1, :],
                           jnp.where(row == 1, prev[SUBLANES - 1:SUBLANES, :], a2))
            carry_s[:, cs] = a[tm - SUBLANES:, :]
            a_ref[0, :, cs] = a[tm - SUBLANES:, :]
        else:
            a1 = a1 + p1_ref[0, :, cs]
            a2 = a2 + p2_ref[0, :, cs]
            a_ref[0, :, cs] = a
        y = cb_ref[:, cs] + cw_ref[0:1, cs] * a2 + cw_ref[1:2, cs] * a1 + cw_ref[2:3, cs] * a
        z_s[:, cs] = (y * jax.nn.sigmoid(y) * bgate).astype(BF16)
    f = jnp.dot(z_s[...], wd_ref[...], preferred_element_type=F32)
    out_ref[0] = x + gt_ref[0] * f


def _conv_ffn(x, sh, sc, gate, g, w_in, conv_w, conv_b, w_down, seq_len, p1=None, p2=None):
    b, t, d = x.shape
    dff = w_down.shape[0]
    tm = _row_tile(t)
    carry_rows = p1 is None
    chunk = 256
    assert dff % chunk == 0 and tm % SUBLANES == 0
    tmod = sh.shape[1]
    tmb = tm if tmod == t else 1
    mod_map = (lambda bi, mi: (bi, mi, 0)) if tmod == t else (lambda bi, mi: (bi, 0, 0))
    row = lambda bi, mi: (bi, mi, 0)
    const = lambda bi, mi: (0, 0)
    if carry_rows:
        assert seq_len == t
        p1 = p2 = jnp.zeros((1, SUBLANES, LANES), F32)
        p_spec = pl.BlockSpec((1, SUBLANES, LANES), lambda bi, mi: (0, 0, 0))
        a_rows = SUBLANES
        a_spec = pl.BlockSpec((1, SUBLANES, dff), lambda bi, mi: (bi, 0, 0))
    else:
        assert tm % seq_len == 0
        p_spec = pl.BlockSpec((1, tm, dff), row)
        a_rows = t
        a_spec = pl.BlockSpec((1, tm, dff), row)
    return pl.pallas_call(
        functools.partial(_conv_ffn_body, chunk=chunk, seq_len=seq_len, carry_rows=carry_rows),
        grid=(b, t // tm),
        in_specs=[pl.BlockSpec((1, tm, d), row),
                  pl.BlockSpec((1, tmb, d), mod_map),
                  pl.BlockSpec((1, tmb, d), mod_map),
                  pl.BlockSpec((1, tmb, d), mod_map),
                  pl.BlockSpec((1, d), const),
                  pl.BlockSpec((d, 2 * dff), const),
                  pl.BlockSpec((CONV_W, dff), const),
                  pl.BlockSpec((1, dff), const),
                  pl.BlockSpec((dff, d), const),
                  p_spec, p_spec],
        out_specs=[pl.BlockSpec((1, tm, d), row), a_spec],
        out_shape=[jax.ShapeDtypeStruct((b, t, d), F32),
                   jax.ShapeDtypeStruct((b, a_rows, dff), F32)],
        scratch_shapes=[pltpu.VMEM((tm, dff), BF16),
                        pltpu.VMEM((SUBLANES, dff), F32)],
        compiler_params=_cparams("parallel", "arbitrary"),
        name="conv_ffn",
    )(x, sh, sc, gate, g, w_in, conv_w, conv_b, w_down, p1, p2)


def _final_norm_body(x_ref, g_ref, o_ref):
    x = x_ref[0]
    ms = jnp.mean(x * x, axis=-1, keepdims=True)
    o_ref[0] = x * lax.rsqrt(ms + RMS_EPS) * g_ref[...]


def _final_norm(x, g):
    b, t, d = x.shape
    tm = _row_tile(t)
    return pl.pallas_call(
        _final_norm_body,
        grid=(b, t // tm),
        in_specs=[pl.BlockSpec((1, tm, d), lambda bi, mi: (bi, mi, 0)),
                  pl.BlockSpec((1, d), lambda bi, mi: (0, 0))],
        out_specs=pl.BlockSpec((1, tm, d), lambda bi, mi: (bi, mi, 0)),
        out_shape=jax.ShapeDtypeStruct((b, t, d), F32),
        compiler_params=_cparams("parallel", "parallel"),
        name="final_norm",
    )(x, g)


def _alibi_slopes(n_heads):
    return jnp.exp2(-8.0 * jnp.arange(1, n_heads + 1, dtype=F32) / n_heads)


def kernel(x_prompt, x_sample, cache_moba_k, cache_moba_v, page_table, state_win1, state_win2, state_win3,
           state_conv, c_prompt, c_sample, ada_w, ada_b, norm1_g, norm2_g, final_g, a_w_qkv, a_w_o,
           b_w_qkv, b_w_o, ffn_w_in, ffn_conv_w, ffn_conv_b, ffn_w_down):
    depth = ada_w.shape[0]
    bp, tp, d = x_prompt.shape
    bs, ts, _ = x_sample.shape
    n_heads_a = d // HEAD_DIM_A
    dff = ffn_w_down.shape[1]
    states = (state_win1, state_win2, state_win3)

    slopes_a = _alibi_slopes(n_heads_a)
    slopes_b = _alibi_slopes(N_GROUPS_B * HEADS_PER_GROUP_B)
    row = jnp.arange(ts * n_heads_a)
    slope_rows = jnp.broadcast_to(slopes_a[row % n_heads_a][:, None], (ts * n_heads_a, MOBA_BLOCK)).astype(F32)
    tq_rows = jnp.broadcast_to((row // n_heads_a)[:, None], (ts * n_heads_a, MOBA_BLOCK)).astype(F32)

    mods = _ada_all(jnp.concatenate([c_prompt, c_sample], axis=0), ada_w, ada_b)
    mods = mods.reshape(depth, bp + bs, 6, d)

    cache_kt = jnp.transpose(cache_moba_k, (0, 1, 3, 4, 2))
    cache_vt = jnp.transpose(cache_moba_v, (0, 1, 3, 4, 2))

    xp = x_prompt
    xs = x_sample.reshape(1, bs * ts, d)
    ka_p, va_p, ka_s, va_s = [], [], [], []
    win_p = [[] for _ in range(N_GROUPS_B)]
    win_new = [[] for _ in range(N_GROUPS_B)]
    conv_p, conv_s = [], []
    for i in range(depth):
        mp = [mods[i, :bp, k].reshape(bp, 1, d) for k in range(6)]
        ms = [jnp.repeat(mods[i, bp:, k], ts, axis=0).reshape(1, bs * ts, d) for k in range(6)]
        g1 = norm1_g[i].reshape(1, d)
        g2 = norm2_g[i].reshape(1, d)
        j = i // 2
        if i % 2 == 0:
            w_o = a_w_o[j].astype(BF16)
            q_p, kb_p, kt_p, vt_p, vtb_p, km_p = _moba_qkv_prompt(xp, mp[0], mp[1], g1, a_w_qkv[j])
            qkv_s = _nm_matmul(xs, ms[0], ms[1], g1, a_w_qkv[j].astype(BF16)).reshape(bs, ts, 3 * d)
            o_p = _moba_prompt_attn(q_p, kb_p, vtb_p, km_p, slopes_a)
            o_s = _moba_sample_attn(qkv_s, cache_kt, cache_vt, j, page_table, slope_rows, tq_rows)
            xp = _oproj([o_p], w_o, xp, mp[2])
            xs = _oproj([o_s.reshape(1, bs * ts, d)], w_o, xs, ms[2])
            ka_p.append(kt_p.reshape(bp, n_heads_a, HEAD_DIM_A, tp))
            va_p.append(vt_p.reshape(bp, n_heads_a, HEAD_DIM_A, tp))
            ka_s.append(qkv_s[:, :, d:2 * d].reshape(bs, ts, n_heads_a, HEAD_DIM_A))
            va_s.append(qkv_s[:, :, 2 * d:].reshape(bs, ts, n_heads_a, HEAD_DIM_A))
        else:
            w_qkv = b_w_qkv[j].astype(BF16)
            w_o = b_w_o[j].astype(BF16)
            qkv_p = _nm_matmul(xp, mp[0], mp[1], g1, w_qkv)
            qkv_s = _nm_matmul(xs, ms[0], ms[1], g1, w_qkv).reshape(bs, ts, -1)
            parts = [_dilated_prompt_group(qkv_p, slopes_b, g) for g in range(N_GROUPS_B)]
            xp = _oproj([p[0] for p in parts] + [p[1] for p in parts], w_o, xp, mp[2])
            o_s = _dilated_sample_attn(qkv_s, states, j, slopes_b)
            xs = _oproj([o_s.reshape(1, bs * ts, D_GROUP_B)], w_o, xs, ms[2])
            kv_s = qkv_s.reshape(bs, ts, 3, N_GROUPS_B, HEADS_PER_GROUP_B, HEAD_DIM_B)[:, :, 1:]
            for g in range(N_GROUPS_B):
                wp = min(WINDOWS[g], tp)
                tail = qkv_p[:, tp - wp:].reshape(bp, wp, 3, N_GROUPS_B, HEADS_PER_GROUP_B, HEAD_DIM_B)
                win_p[g].append(tail[:, :, 1:, g])
                win_new[g].append(kv_s[:, :, :, g])
        w_in = ffn_w_in[i].astype(BF16)
        w_down = ffn_w_down[i].astype(BF16)
        cw = ffn_conv_w[i]
        cb = ffn_conv_b[i].reshape(1, dff)
        xp, a_tail = _conv_ffn(xp, mp[3], mp[4], mp[5], g2, w_in, cw, cb, w_down, tp)
        conv_p.append(a_tail[:, -(CONV_W - 1):])
        st = state_conv[i]
        zero = jnp.zeros((bs, 1, dff), F32)
        p1 = jnp.concatenate([st[:, 1:2], jnp.tile(zero, (1, ts - 1, 1))], axis=1).reshape(1, bs * ts, dff)
        p2 = jnp.concatenate([st[:, 0:1], st[:, 1:2], jnp.tile(zero, (1, ts - 2, 1))], axis=1).reshape(1, bs * ts, dff)
        xs, a_s = _conv_ffn(xs, ms[3], ms[4], ms[5], g2, w_in, cw, cb, w_down, ts, p1, p2)
        a_ext = jnp.concatenate([st, a_s.reshape(bs, ts, dff)], axis=1)
        conv_s.append(a_ext[:, -(CONV_W - 1):])
    y_prompt = _final_norm(xp, final_g.reshape(1, d))
    y_sample = _final_norm(xs, final_g.reshape(1, d)).reshape(bs, ts, d)
    win_s = _win_update(states, [jnp.stack(w) for w in win_new])
    moba_k_prompt = jnp.transpose(jnp.stack(ka_p), (0, 1, 4, 2, 3))
    moba_v_prompt = jnp.transpose(jnp.stack(va_p), (0, 1, 4, 2, 3))
    return (y_prompt, y_sample, moba_k_prompt, moba_v_prompt, jnp.stack(ka_s), jnp.stack(va_s),
            jnp.stack(win_p[0]), jnp.stack(win_p[1]), jnp.stack(win_p[2]),
            win_s[0], win_s[1], win_s[2],
            jnp.stack(conv_p), jnp.stack(conv_s))
```

```python
import functools

import jax
import jax.numpy as jnp
from jax import lax
from jax.experimental import pallas as pl
from jax.experimental.pallas import tpu as pltpu

F32 = jnp.float32
BF16 = jnp.bfloat16

HEAD_DIM_A = 64
MOBA_BLOCK = 256
MOBA_TOPK = 3
HEAD_DIM_B = 128
N_GROUPS_B = 3
HEADS_PER_GROUP_B = 4
D_GROUP_B = HEADS_PER_GROUP_B * HEAD_DIM_B
WINDOWS = (128, 512, 2048)
DILATIONS = (1, 4, 16)
CONV_W = 3
RMS_EPS = 1e-6
NEG_INF = -1e30
POS_BIG = 1e30
LOG2E = 1.4426950408889634

LANES = 128
SUBLANES = 8
VMEM_LIMIT = 56 * 1024 * 1024

MOBA_HEADS_PER_STEP = 8
MOBA_BLOCKS_PER_ITER = 2
MOBA_PAGES_PER_STEP = 4

_NT = (((1,), (1,)), ((), ()))


def _cparams(*sem):
    return pltpu.CompilerParams(dimension_semantics=sem, vmem_limit_bytes=VMEM_LIMIT)


def _row_tile(t, cap=512):
    tm = min(t, cap)
    assert t % tm == 0
    return tm


def _norm_mod(x, g, sh, sc):
    ms = jnp.mean(x * x, axis=-1, keepdims=True)
    y = x * lax.rsqrt(ms + RMS_EPS) * g
    return y * (1.0 + sc) + sh


def _ada_body(c_ref, w_ref, b_ref, o_ref):
    c = c_ref[...]
    s = (c * jax.nn.sigmoid(c)).astype(BF16)
    o_ref[0] = jnp.dot(s, w_ref[0].astype(BF16), preferred_element_type=F32) + b_ref[0]


def _ada_all(c_all, ada_w, ada_b):
    depth, d, n6 = ada_w.shape
    nb = c_all.shape[0]
    tn = n6 // 4
    return pl.pallas_call(
        _ada_body,
        grid=(depth, n6 // tn),
        in_specs=[pl.BlockSpec((nb, d), lambda l, n: (0, 0)),
                  pl.BlockSpec((1, d, tn), lambda l, n: (l, 0, n)),
                  pl.BlockSpec((1, 1, tn), lambda l, n: (l, 0, n))],
        out_specs=pl.BlockSpec((1, nb, tn), lambda l, n: (l, 0, n)),
        out_shape=jax.ShapeDtypeStruct((depth, nb, n6), F32),
        compiler_params=_cparams("parallel", "parallel"),
        name="ada_table",
    )(c_all, ada_w, ada_b.reshape(depth, 1, n6))


def _nm_matmul_body(x_ref, sh_ref, sc_ref, g_ref, w_ref, o_ref, h_s):
    @pl.when(pl.program_id(2) == 0)
    def _():
        h_s[...] = _norm_mod(x_ref[0], g_ref[...], sh_ref[0], sc_ref[0]).astype(BF16)

    o_ref[0] = jnp.dot(h_s[...], w_ref[...], preferred_element_type=F32)


def _nm_matmul(x, sh, sc, g, w, n_tiles=3):
    b, t, d = x.shape
    n = w.shape[1]
    tm = _row_tile(t)
    tmod = sh.shape[1]
    tmb = tm if tmod == t else 1
    mod_map = (lambda bi, mi, ni: (bi, mi, 0)) if tmod == t else (lambda bi, mi, ni: (bi, 0, 0))
    tn = n // n_tiles
    return pl.pallas_call(
        _nm_matmul_body,
        grid=(b, t // tm, n_tiles),
        in_specs=[pl.BlockSpec((1, tm, d), lambda bi, mi, ni: (bi, mi, 0)),
                  pl.BlockSpec((1, tmb, d), mod_map),
                  pl.BlockSpec((1, tmb, d), mod_map),
                  pl.BlockSpec((1, d), lambda bi, mi, ni: (0, 0)),
                  pl.BlockSpec((d, tn), lambda bi, mi, ni: (0, ni))],
        out_specs=pl.BlockSpec((1, tm, tn), lambda bi, mi, ni: (bi, mi, ni)),
        out_shape=jax.ShapeDtypeStruct((b, t, n), F32),
        scratch_shapes=[pltpu.VMEM((tm, d), BF16)],
        compiler_params=_cparams("parallel", "parallel", "arbitrary"),
        name="norm_mod_matmul",
    )(x, sh, sc, g, w)


def _moba_qkv_body(x_ref, sh_ref, sc_ref, g_ref, wq_ref, wk_ref, wkt_ref, wvt_ref,
                   q_ref, kb_ref, kt_ref, vt_ref, vtb_ref, km_ref):
    tm = x_ref.shape[1]
    per_tile = tm // MOBA_BLOCK
    h = _norm_mod(x_ref[0], g_ref[...], sh_ref[0], sc_ref[0]).astype(BF16)
    q_ref[0] = jnp.dot(h, wq_ref[...], preferred_element_type=F32)
    k = jnp.dot(h, wk_ref[...], preferred_element_type=F32)
    kb_ref[0] = k.astype(BF16)
    first = pl.program_id(1) * per_tile
    for j in range(per_tile):
        km_ref[0, pl.ds(first + j, 1), :] = (
            jnp.sum(k[j * MOBA_BLOCK:(j + 1) * MOBA_BLOCK], axis=0, keepdims=True) * (1.0 / MOBA_BLOCK))
    kt_ref[0] = lax.dot_general(wkt_ref[...], h, _NT, preferred_element_type=F32)
    vt = lax.dot_general(wvt_ref[...], h, _NT, preferred_element_type=F32)
    vt_ref[0] = vt
    for j in range(per_tile):
        vtb_ref[0, j] = vt[:, j * MOBA_BLOCK:(j + 1) * MOBA_BLOCK].astype(BF16)


def _moba_qkv_prompt(x, sh, sc, g, w_qkv):
    b, t, d = x.shape
    tm = _row_tile(t)
    assert tm % MOBA_BLOCK == 0
    nb = t // MOBA_BLOCK
    wq = w_qkv[:, :d].astype(BF16)
    wk = w_qkv[:, d:2 * d].astype(BF16)
    wkt = w_qkv[:, d:2 * d].T.astype(BF16)
    wvt = w_qkv[:, 2 * d:].T.astype(BF16)
    row = lambda bi, mi: (bi, mi, 0)
    col = lambda bi, mi: (bi, 0, mi)
    mod = lambda bi, mi: (bi, 0, 0)
    const = lambda bi, mi: (0, 0)
    return pl.pallas_call(
        _moba_qkv_body,
        grid=(b, t // tm),
        in_specs=[pl.BlockSpec((1, tm, d), row),
                  pl.BlockSpec((1, 1, d), mod),
                  pl.BlockSpec((1, 1, d), mod),
                  pl.BlockSpec((1, d), const)] + [pl.BlockSpec((d, d), const)] * 4,
        out_specs=[pl.BlockSpec((1, tm, d), row),
                   pl.BlockSpec((1, tm, d), row),
                   pl.BlockSpec((1, d, tm), col),
                   pl.BlockSpec((1, d, tm), col),
                   pl.BlockSpec((1, tm // MOBA_BLOCK, d, MOBA_BLOCK), lambda bi, mi: (bi, mi, 0, 0)),
                   pl.BlockSpec((1, nb, d), mod)],
        out_shape=[jax.ShapeDtypeStruct((b, t, d), F32),
                   jax.ShapeDtypeStruct((b, t, d), BF16),
                   jax.ShapeDtypeStruct((b, d, t), F32),
                   jax.ShapeDtypeStruct((b, d, t), F32),
                   jax.ShapeDtypeStruct((b, nb, d, MOBA_BLOCK), BF16),
                   jax.ShapeDtypeStruct((b, nb, d), F32)],
        compiler_params=_cparams("parallel", "arbitrary"),
        name="moba_qkv_prompt",
    )(x, sh, sc, g, wq, wk, wkt, wvt)


def _topk_rank_select(gate, blk, n_valid, n_rows):
    gate = jnp.where(blk < n_valid, gate, NEG_INF)
    rank = jnp.zeros(gate.shape, F32)
    for m in range(n_rows):
        gm = gate[m:m + 1, :]
        beats = (gm > gate) | ((gm == gate) & (blk > m))
        rank = rank + jnp.where(beats, 1.0, 0.0)
    n_sel = jnp.minimum(n_valid, MOBA_TOPK).astype(F32)
    return jnp.where((rank < n_sel) & (blk < n_valid), 1.0, 0.0)


def _moba_prompt_body(slopes_ref, q_ref, kb_ref, vtb_ref, km_ref, o_ref,
                      d0_s, sel_s, qb_s, m_s, l_s, acc_s, s_s, *, nb, hps, per_iter):
    blk_sz, hd = MOBA_BLOCK, HEAD_DIM_A
    head0 = pl.program_id(1) * hps
    i = pl.program_id(2)

    @pl.when(i == 0)
    def _init():
        rk = lax.broadcasted_iota(jnp.int32, (blk_sz, blk_sz), 0)
        rq = lax.broadcasted_iota(jnp.int32, (blk_sz, blk_sz), 1)
        rel = (rk - rq).astype(F32)
        for h in range(hps):
            d0 = (slopes_ref[head0 + h] * LOG2E) * rel
            d0_s[h, 0] = d0
            d0_s[h, 1] = jnp.where(rk <= rq, d0, NEG_INF)

    lane = lax.broadcasted_iota(jnp.int32, (1, LANES), 1)
    blk = lax.broadcasted_iota(jnp.int32, (nb, blk_sz), 0)
    for h in range(hps):
        pair, hh = divmod(h, LANES // hd)
        cs = slice(pair * LANES, (pair + 1) * LANES)
        qh = jnp.where((lane >= hh * hd) & (lane < (hh + 1) * hd), q_ref[0, :, cs] * (hd ** -0.5 * LOG2E), 0.0)
        gate = lax.dot_general(km_ref[0, :, cs], qh, _NT, precision=lax.Precision.HIGHEST,
                               preferred_element_type=F32)
        sel_s[h] = jnp.where(blk == i, 1.0, _topk_rank_select(gate, blk, i, nb))
        qb_s[h] = qh.astype(BF16)
        m_s[h] = jnp.full((1, blk_sz), NEG_INF, F32)
        l_s[h] = jnp.zeros((1, blk_sz), F32)
        acc_s[h] = jnp.zeros((hd, blk_sz), F32)

    def key_blocks(it, carry):
        blocks = []
        for u in range(per_iter):
            n = it * per_iter + u
            n_mem = jnp.minimum(n, nb - 1)
            rows = pl.ds(pl.multiple_of(n_mem * blk_sz, blk_sz), blk_sz)
            for h in range(hps):
                pair = h // (LANES // hd)
                kblk = kb_ref[0, rows, pair * LANES:(pair + 1) * LANES]
                s_s[u, h] = lax.dot_general(kblk, qb_s[h], _NT, preferred_element_type=F32)
            blocks.append((n, n_mem))
        for u, (n, n_mem) in enumerate(blocks):
            off = ((i - n) * blk_sz).astype(F32)
            own = (n == i).astype(jnp.int32)
            for h in range(hps):
                s0 = s_s[u, h] + d0_s[h, own]
                c_n = -(slopes_ref[head0 + h] * LOG2E) * off
                selrow = (sel_s[h, pl.ds(n_mem, 1), :] > 0.5) & (n <= i)
                m_old = m_s[h]
                m_new = jnp.maximum(m_old, jnp.where(selrow, jnp.max(s0, axis=0, keepdims=True) + c_n, NEG_INF))
                alpha = jnp.exp2(m_old - m_new)
                p = jnp.exp2(s0 - jnp.where(selrow, m_new - c_n, POS_BIG))
                l_s[h] = alpha * l_s[h] + jnp.sum(p, axis=0, keepdims=True)
                vt = vtb_ref[0, n_mem, h * hd:(h + 1) * hd, :]
                acc_s[h] = alpha * acc_s[h] + jnp.dot(vt, p.astype(BF16), preferred_element_type=F32)
                m_s[h] = m_new
        return carry

    lax.fori_loop(0, (i + per_iter) // per_iter, key_blocks, 0)
    per_pair = LANES // hd
    for pair in range(hps // per_pair):
        outs = [acc_s[pair * per_pair + hh] / l_s[pair * per_pair + hh] for hh in range(per_pair)]
        o_ref[0, :, pair * LANES:(pair + 1) * LANES] = jnp.concatenate(outs, axis=0).T.astype(o_ref.dtype)


def _moba_prompt_attn(q, kb, vtb, km, slopes):
    b, t, d = q.shape
    nb = t // MOBA_BLOCK
    hps = MOBA_HEADS_PER_STEP
    w = hps * HEAD_DIM_A
    assert t % MOBA_BLOCK == 0 and d % w == 0 and w % LANES == 0
    return pl.pallas_call(
        functools.partial(_moba_prompt_body, nb=nb, hps=hps, per_iter=MOBA_BLOCKS_PER_ITER),
        grid=(b, d // w, nb),
        in_specs=[pl.BlockSpec(memory_space=pltpu.SMEM),
                  pl.BlockSpec((1, MOBA_BLOCK, w), lambda bi, hg, i: (bi, i, hg)),
                  pl.BlockSpec((1, t, w), lambda bi, hg, i: (bi, 0, hg)),
                  pl.BlockSpec((1, nb, w, MOBA_BLOCK), lambda bi, hg, i: (bi, 0, hg, 0)),
                  pl.BlockSpec((1, nb, w), lambda bi, hg, i: (bi, 0, hg))],
        out_specs=pl.BlockSpec((1, MOBA_BLOCK, w), lambda bi, hg, i: (bi, i, hg)),
        out_shape=jax.ShapeDtypeStruct((b, t, d), BF16),
        scratch_shapes=[pltpu.VMEM((hps, 2, MOBA_BLOCK, MOBA_BLOCK), F32),
                        pltpu.VMEM((hps, nb, MOBA_BLOCK), F32),
                        pltpu.VMEM((hps, MOBA_BLOCK, LANES), BF16),
                        pltpu.VMEM((hps, 1, MOBA_BLOCK), F32),
                        pltpu.VMEM((hps, 1, MOBA_BLOCK), F32),
                        pltpu.VMEM((hps, HEAD_DIM_A, MOBA_BLOCK), F32),
                        pltpu.VMEM((MOBA_BLOCKS_PER_ITER, hps, MOBA_BLOCK, MOBA_BLOCK), F32)],
        compiler_params=_cparams("parallel", "parallel", "arbitrary"),
        name="moba_prompt_attn",
    )(slopes, q, kb, vtb, km)


def _head_block_diag(q, n_heads, hd):
    d = n_heads * hd
    n_t = LANES // n_heads
    row_h = lax.broadcasted_iota(jnp.int32, (n_heads, d), 0)
    col_h = lax.broadcasted_iota(jnp.int32, (n_heads, d), 1) // hd
    pieces = []
    for t in range(n_t):
        if t < q.shape[0]:
            pieces.append(jnp.where(row_h == col_h, jnp.broadcast_to(q[t:t + 1, :], (n_heads, d)), 0.0))
        else:
            pieces.append(jnp.zeros((n_heads, d), F32))
    return jnp.concatenate(pieces, axis=0)


def _moba_s_scores_body(pt_ref, q_ref, *refs, n_heads, t_dec, pages):
    k_refs, s_ref, qs_s = refs[:pages], refs[pages], refs[pages + 1]
    hd = HEAD_DIM_A
    d = n_heads * hd
    rows = t_dec * n_heads
    page = k_refs[0].shape[-1]
    per_blk = MOBA_BLOCK // page

    @pl.when(pl.program_id(1) == 0)
    def _():
        qbd = _head_block_diag(q_ref[0] * (hd ** -0.5), n_heads, hd)[:rows]
        hi = qbd.astype(BF16)
        qs_s[0:rows] = hi
        qs_s[rows:2 * rows] = (qbd - hi.astype(F32)).astype(BF16)

    for p, k_ref in enumerate(k_refs):
        lanes = slice((p % per_blk) * page, (p % per_blk + 1) * page)
        kp = k_ref[0, 0].reshape(d, page)
        k_hi = kp.astype(BF16)
        k_lo = (kp - k_hi.astype(F32)).astype(BF16)
        a = jnp.dot(qs_s[...], k_hi, preferred_element_type=F32)
        b = jnp.dot(qs_s[0:rows], k_lo, preferred_element_type=F32)
        s_ref[0, p // per_blk, :, lanes] = a[:rows] + a[rows:] + b


def _moba_s_softmax_body(s_ref, q_ref, kn_ref, sl_ref, tq_ref, p_ref, po_ref, l_ref,
                         *, n_heads, n_blocks, t_dec, past_len):
    hd = HEAD_DIM_A
    rows = t_dec * n_heads
    blk = MOBA_BLOCK
    lane = lax.broadcasted_iota(jnp.int32, (rows, LANES), 1)
    gate = jnp.full((rows, LANES), NEG_INF, F32)
    for n in range(n_blocks):
        gate = jnp.where(lane == n, jnp.sum(s_ref[0, n], axis=1, keepdims=True), gate)
    rank = jnp.zeros((rows, LANES), F32)
    for m in range(n_blocks):
        gm = gate[:, m:m + 1]
        beats = (gm > gate) | ((gm == gate) & (lane > m))
        rank = rank + jnp.where(beats, 1.0, 0.0)
    sel = (rank < min(MOBA_TOPK, n_blocks)) & (lane < n_blocks)

    sl = sl_ref[...]
    tq = tq_ref[...]
    tok = lax.broadcasted_iota(jnp.int32, (rows, blk), 1).astype(F32)
    dist0 = past_len + tq - tok
    sl1, tq1 = sl[:, 0:1], tq[:, 0:1]

    qbd = _head_block_diag(q_ref[0] * (hd ** -0.5), n_heads, hd)[:rows]
    kn = kn_ref[0]
    own = []
    for tk in range(t_dec):
        s = jnp.sum(qbd * kn[tk:tk + 1, :], axis=1, keepdims=True)
        dd = tq1 - tk
        own.append(jnp.where(dd >= 0, s - sl1 * dd, NEG_INF))
    m = own[0]
    for s in own[1:]:
        m = jnp.maximum(m, s)
    for n in range(n_blocks):
        sb = s_ref[0, n] - sl * (dist0 - n * blk)
        m = jnp.maximum(m, jnp.where(sel[:, n:n + 1], jnp.max(sb, axis=1, keepdims=True), NEG_INF))
    l = jnp.zeros((rows, 1), F32)
    for n in range(n_blocks):
        sb = s_ref[0, n] - sl * (dist0 - n * blk)
        p = jnp.exp(sb - jnp.where(sel[:, n:n + 1], m, POS_BIG))
        l = l + jnp.sum(p, axis=1, keepdims=True)
        p_ref[0, n, 0:rows, :] = p.astype(BF16)
        p_ref[0, n, rows:, :] = jnp.zeros((p_ref.shape[2] - rows, blk), BF16)
    po = jnp.zeros((rows, LANES), F32)
    for tk in range(t_dec):
        pk = jnp.exp(own[tk] - m)
        l = l + pk
        po = jnp.where(lane == tk, pk, po)
    po_ref[0] = po
    l_ref[0] = jnp.broadcast_to(l, (rows, LANES))


def _moba_s_pv_body(pt_ref, p_ref, po_ref, l_ref, vn_ref, *refs, n_heads, t_dec, pages):
    v_refs, o_ref, acc_s = refs[:pages], refs[pages], refs[pages + 1]
    hd = HEAD_DIM_A
    n = pl.program_id(1)
    page = v_refs[0].shape[-1]
    per_blk = MOBA_BLOCK // page
    d = n_heads * hd
    rows = t_dec * n_heads

    @pl.when(n == 0)
    def _():
        acc_s[...] = jnp.zeros(acc_s.shape, F32)

    tot = None
    for p, v_ref in enumerate(v_refs):
        vt = v_ref[0, 0].reshape(d, page).astype(BF16)
        pp = p_ref[0, p // per_blk][:, (p % per_blk) * page:(p % per_blk + 1) * page]
        y = lax.dot_general(vt, pp, _NT, preferred_element_type=F32)
        tot = y if tot is None else tot + y
    acc_s[...] += tot

    @pl.when(n == pl.num_programs(1) - 1)
    def _():
        acc = acc_s[...].T[:rows]
        po = po_ref[0]
        vn = vn_ref[0]
        for tk in range(t_dec):
            acc = acc + po[:, tk:tk + 1] * vn[tk:tk + 1, :]
        row_h = lax.broadcasted_iota(jnp.int32, (rows, d), 0) % n_heads
        col_h = lax.broadcasted_iota(jnp.int32, (rows, d), 1) // hd
        acc = jnp.where(row_h == col_h, acc / l_ref[0][:, 0:1], 0.0)
        for t in range(t_dec):
            o_ref[0, t:t + 1, :] = jnp.sum(acc[t * n_heads:(t + 1) * n_heads, :], axis=0, keepdims=True)


def _moba_sample_attn(qkv_s, cache_kt, cache_vt, layer, page_table, slope_rows, tq_rows):
    bs, t_dec, d3 = qkv_s.shape
    d = d3 // 3
    n_heads, hd, page = cache_kt.shape[2:]
    n_pages = page_table.shape[1]
    past_len = n_pages * page
    pages = MOBA_PAGES_PER_STEP
    rows = t_dec * n_heads
    assert hd == HEAD_DIM_A and n_heads * hd == d
    assert MOBA_BLOCK % page == 0 and past_len % MOBA_BLOCK == 0 and page == LANES
    assert rows % SUBLANES == 0 and rows <= LANES and t_dec <= SUBLANES
    n_blocks = past_len // MOBA_BLOCK
    blocks_per_step = pages * page // MOBA_BLOCK
    assert MOBA_TOPK <= n_blocks <= LANES and n_pages % pages == 0 and (pages * page) % MOBA_BLOCK == 0
    n_steps = n_pages // pages
    pad = SUBLANES - t_dec
    q8 = jnp.pad(qkv_s[:, :, :d], ((0, 0), (0, pad), (0, 0)))
    k8 = jnp.pad(qkv_s[:, :, d:2 * d], ((0, 0), (0, pad), (0, 0)))
    v8 = jnp.pad(qkv_s[:, :, 2 * d:], ((0, 0), (0, pad), (0, 0)))

    def page_spec(p):
        return pl.BlockSpec((1, 1, n_heads, hd, page),
                            lambda bi, n, pt: (layer, pt[bi, pages * n + p], 0, 0, 0))

    small = lambda bi, n, pt: (bi, 0, 0)
    s_all = pl.pallas_call(
        functools.partial(_moba_s_scores_body, n_heads=n_heads, t_dec=t_dec, pages=pages),
        grid_spec=pltpu.PrefetchScalarGridSpec(
            num_scalar_prefetch=1,
            grid=(bs, n_steps),
            in_specs=[pl.BlockSpec((1, SUBLANES, d), small)] + [page_spec(p) for p in range(pages)],
            out_specs=pl.BlockSpec((1, blocks_per_step, rows, MOBA_BLOCK), lambda bi, n, pt: (bi, n, 0, 0)),
            scratch_shapes=[pltpu.VMEM((2 * rows, d), BF16)]),
        out_shape=jax.ShapeDtypeStruct((bs, n_blocks, rows, MOBA_BLOCK), F32),
        compiler_params=_cparams("parallel", "arbitrary"),
        name="moba_sample_scores",
    )(page_table, q8, *([cache_kt] * pages))

    per_b = lambda bi: (bi, 0, 0)
    const = lambda bi: (0, 0)
    p_all, p_own, l_sum = pl.pallas_call(
        functools.partial(_moba_s_softmax_body, n_heads=n_heads, n_blocks=n_blocks, t_dec=t_dec,
                          past_len=past_len),
        grid=(bs,),
        in_specs=[pl.BlockSpec((1, n_blocks, rows, MOBA_BLOCK), lambda bi: (bi, 0, 0, 0)),
                  pl.BlockSpec((1, SUBLANES, d), per_b),
                  pl.BlockSpec((1, SUBLANES, d), per_b),
                  pl.BlockSpec((rows, MOBA_BLOCK), const),
                  pl.BlockSpec((rows, MOBA_BLOCK), const)],
        out_specs=[pl.BlockSpec((1, n_blocks, LANES, MOBA_BLOCK), lambda bi: (bi, 0, 0, 0)),
                   pl.BlockSpec((1, rows, LANES), per_b),
                   pl.BlockSpec((1, rows, LANES), per_b)],
        out_shape=[jax.ShapeDtypeStruct((bs, n_blocks, LANES, MOBA_BLOCK), BF16),
                   jax.ShapeDtypeStruct((bs, rows, LANES), F32),
                   jax.ShapeDtypeStruct((bs, rows, LANES), F32)],
        compiler_params=_cparams("parallel"),
        name="moba_sample_softmax",
    )(s_all, q8, k8, slope_rows, tq_rows)

    return pl.pallas_call(
        functools.partial(_moba_s_pv_body, n_heads=n_heads, t_dec=t_dec, pages=pages),
        grid_spec=pltpu.PrefetchScalarGridSpec(
            num_scalar_prefetch=1,
            grid=(bs, n_steps),
            in_specs=[pl.BlockSpec((1, blocks_per_step, LANES, MOBA_BLOCK), lambda bi, n, pt: (bi, n, 0, 0)),
                      pl.BlockSpec((1, rows, LANES), small),
                      pl.BlockSpec((1, rows, LANES), small),
                      pl.BlockSpec((1, SUBLANES, d), small)] + [page_spec(p) for p in range(pages)],
            out_specs=pl.BlockSpec((1, t_dec, d), small),
            scratch_shapes=[pltpu.VMEM((d, LANES), F32)]),
        out_shape=jax.ShapeDtypeStruct((bs, t_dec, d), F32),
        compiler_params=_cparams("parallel", "arbitrary"),
        name="moba_sample_pv",
    )(page_table, p_all, p_own, l_sum, v8, *([cache_vt] * pages))


def _dilated_prompt_body(slopes_ref, q_ref, kp_ref, kc_ref, vp_ref, vc_ref, o_ref, lse_ref, bias_s,
                         *, group, dilation, n_keys):
    tq = q_ref.shape[1]
    hd = HEAD_DIM_B
    first = (pl.program_id(0) == 0) & (pl.program_id(1) == 0) & (pl.program_id(2) == 0)

    @pl.when(first)
    def _():
        iq = lax.broadcasted_iota(jnp.int32, (tq, 2 * tq), 0)
        jk = lax.broadcasted_iota(jnp.int32, (tq, 2 * tq), 1)
        steps = iq + tq - jk
        ok = (steps >= 0) & (steps < n_keys)
        dist = (steps * dilation).astype(F32)
        for h in range(HEADS_PER_GROUP_B):
            b = -slopes_ref[group * HEADS_PER_GROUP_B + h] * dist
            bias_s[0, h] = jnp.where(ok & (jk >= tq), b, NEG_INF)
            bias_s[1, h] = jnp.where(ok, b, NEG_INF)

    var = jnp.minimum(pl.program_id(2), 1)
    scale = hd ** -0.5
    def scores(h):
        cs = slice(h * hd, (h + 1) * hd)
        qh = q_ref[0, :, cs].astype(BF16)
        return (lax.dot_general(qh, kp_ref[0, :, cs].astype(BF16), _NT, preferred_element_type=F32),
                lax.dot_general(qh, kc_ref[0, :, cs].astype(BF16), _NT, preferred_element_type=F32))

    s_next = scores(0)
    for h in range(HEADS_PER_GROUP_B):
        cs = slice(h * hd, (h + 1) * hd)
        bias = bias_s[var, h]
        s_p = s_next[0] * scale + bias[:, :tq]
        s_c = s_next[1] * scale + bias[:, tq:]
        if h + 1 < HEADS_PER_GROUP_B:
            s_next = scores(h + 1)
        m = jnp.maximum(jnp.max(s_p, axis=1, keepdims=True), jnp.max(s_c, axis=1, keepdims=True))
        p_p = jnp.exp(s_p - m)
        p_c = jnp.exp(s_c - m)
        l = jnp.sum(p_p, axis=1, keepdims=True) + jnp.sum(p_c, axis=1, keepdims=True)
        o = (jnp.dot(p_p.astype(BF16), vp_ref[0, :, cs].astype(BF16), preferred_element_type=F32)
             + jnp.dot(p_c.astype(BF16), vc_ref[0, :, cs].astype(BF16), preferred_element_type=F32))
        o_ref[0, :, cs] = o / l
        lse_ref[0, :, cs] = jnp.broadcast_to(m + jnp.log(l), (tq, hd))


def _dilated_qkv_body(x_ref, sh_ref, sc_ref, g_ref, w_ref, o0_ref, o1_ref, o2_ref, h_s, r_s):
    grp = pl.program_id(2)
    tm = x_ref.shape[1]

    @pl.when(grp == 0)
    def _():
        h_s[...] = _norm_mod(x_ref[0], g_ref[...], sh_ref[0], sc_ref[0]).astype(BF16)

    res = jnp.dot(h_s[...], w_ref[...], preferred_element_type=F32)
    width = res.shape[1]
    for g, o_ref in enumerate((o0_ref, o1_ref, o2_ref)):
        dil = DILATIONS[g]

        @pl.when(grp == g)
        def _(o_ref=o_ref, dil=dil):
            if dil == 1:
                o_ref[0] = res
            else:
                for c in range(width // LANES):
                    r_s[c] = res[:, c * LANES:(c + 1) * LANES]
                for r in range(dil):
                    for c in range(width // LANES):
                        o_ref[0, :, r * width + c * LANES:r * width + (c + 1) * LANES] = (
                            r_s[c, pl.ds(r, tm // dil, stride=dil), :])


def _dilated_qkv_prompt(x, sh, sc, g, w_qkv):
    b, t, d = x.shape
    dg = D_GROUP_B
    tm = _row_tile(t)
    assert all(tm % dil == 0 and (tm // dil) % SUBLANES == 0 for dil in DILATIONS)
    w = w_qkv.reshape(d, 3, N_GROUPS_B, dg).transpose(0, 2, 1, 3).reshape(d, N_GROUPS_B * 3 * dg).astype(BF16)
    mod = lambda bi, mi, gi: (bi, 0, 0)
    return pl.pallas_call(
        _dilated_qkv_body,
        grid=(b, t // tm, N_GROUPS_B),
        in_specs=[pl.BlockSpec((1, tm, d), lambda bi, mi, gi: (bi, mi, 0)),
                  pl.BlockSpec((1, 1, d), mod),
                  pl.BlockSpec((1, 1, d), mod),
                  pl.BlockSpec((1, d), lambda bi, mi, gi: (0, 0)),
                  pl.BlockSpec((d, 3 * dg), lambda bi, mi, gi: (0, gi))],
        out_specs=[pl.BlockSpec((1, tm // dil, dil * 3 * dg), lambda bi, mi, gi: (bi, mi, 0))
                   for dil in DILATIONS],
        out_shape=[jax.ShapeDtypeStruct((b, t // dil, dil * 3 * dg), F32) for dil in DILATIONS],
        scratch_shapes=[pltpu.VMEM((tm, d), BF16), pltpu.VMEM((3 * dg // LANES, tm, LANES), F32)],
        compiler_params=_cparams("parallel", "arbitrary", "arbitrary"),
        name="dilated_qkv_prompt",
    )(x, sh, sc, g, w)


def _dilated_prompt_group(view, slopes, group):
    dg = D_GROUP_B
    dil = DILATIONS[group]
    b, tr, _ = view.shape
    t = tr * dil
    n_keys = WINDOWS[group] // dil + 1
    tq = n_keys - 1
    assert tr % tq == 0 and tq % LANES == 0

    def sec(section, prev):
        def index(bi, r, ti):
            row = jnp.maximum(ti - 1, 0) if prev else ti
            return (bi, row, r * 3 + section)
        return pl.BlockSpec((1, tq, dg), index)

    out_spec = pl.BlockSpec((1, tq, dg), lambda bi, r, ti: (bi, ti, r))
    o, lse = pl.pallas_call(
        functools.partial(_dilated_prompt_body, group=group, dilation=dil, n_keys=n_keys),
        grid=(b, dil, tr // tq),
        in_specs=[pl.BlockSpec(memory_space=pltpu.SMEM),
                  sec(0, False), sec(1, True), sec(1, False), sec(2, True), sec(2, False)],
        out_specs=[out_spec, out_spec],
        out_shape=[jax.ShapeDtypeStruct((b, tr, dil * dg), F32)] * 2,
        scratch_shapes=[pltpu.VMEM((2, HEADS_PER_GROUP_B, tq, 2 * tq), F32)],
        compiler_params=_cparams("arbitrary", "arbitrary", "arbitrary"),
        name=f"dilated_prompt_g{group}",
    )(slopes, view, view, view, view, view)
    return o.reshape(b, t, dg), lse.reshape(b, t, dg)


def _dilated_sample_body(slopes_ref, x_ref, *refs, t_dec):
    hd = HEAD_DIM_B
    dg = D_GROUP_B
    scale = hd ** -0.5
    x = x_ref[0]
    rows = x.shape[0]
    tq = lax.broadcasted_iota(jnp.int32, (rows, LANES), 0)
    mcol = lax.broadcasted_iota(jnp.int32, (rows, LANES), 1)
    tq1 = lax.broadcasted_iota(jnp.int32, (rows, 1), 0)
    o_ref = refs[-1]
    bufs, pos = [], 0
    for g in range(N_GROUPS_B):
        n_res = min(DILATIONS[g], t_dec)
        bufs.append(refs[pos:pos + n_res])
        pos += n_res
    for h in range(HEADS_PER_GROUP_B):
        outs, lses = [], []
        for g in range(N_GROUPS_B):
            dil = DILATIONS[g]
            n_back = WINDOWS[g] // dil
            slope = slopes_ref[g * HEADS_PER_GROUP_B + h]
            c0 = g * dg + h * hd
            qh = x[:, c0:c0 + hd]
            kn = x[:, N_GROUPS_B * dg + c0:N_GROUPS_B * dg + c0 + hd]
            vn = x[:, 2 * N_GROUPS_B * dg + c0:2 * N_GROUPS_B * dg + c0 + hd]
            qb = qh.astype(BF16)
            s_list, v_list = [], []
            for r in range(min(dil, t_dec)):
                kr = bufs[g][r][0, :, h * hd:(h + 1) * hd]
                v_list.append(bufs[g][r][0, :, dg + h * hd:dg + (h + 1) * hd])
                num = dil * (n_back - mcol) + (tq - r)
                ok = (num >= 0) & (num <= dil * n_back) & ((num & (dil - 1)) == 0)
                s = lax.dot_general(qb, kr.astype(BF16), _NT, preferred_element_type=F32) * scale
                s_list.append(jnp.where(ok, s - slope * num.astype(F32), NEG_INF))
            s_new = []
            for tk in range(t_dec):
                num = tq1 - tk
                ok = (num >= 0) & (num <= dil * n_back) & ((num & (dil - 1)) == 0)
                s = jnp.sum(qh * kn[tk:tk + 1, :], axis=1, keepdims=True) * scale
                s_new.append(jnp.where(ok, s - slope * num.astype(F32), NEG_INF))
            m = s_new[0]
            for s in s_new[1:]:
                m = jnp.maximum(m, s)
            for s in s_list:
                m = jnp.maximum(m, jnp.max(s, axis=1, keepdims=True))
            l = jnp.zeros((rows, 1), F32)
            o = jnp.zeros((rows, hd), F32)
            for s, v in zip(s_list, v_list):
                p = jnp.exp(s - m)
                l = l + jnp.sum(p, axis=1, keepdims=True)
                o = o + jnp.dot(p.astype(BF16), v.astype(BF16), preferred_element_type=F32)
            for tk, s in enumerate(s_new):
                p = jnp.exp(s - m)
                l = l + p
                o = o + p * vn[tk:tk + 1, :]
            outs.append(o / l)
            lses.append(m + jnp.log(l))
        lm = jnp.maximum(jnp.maximum(lses[0], lses[1]), lses[2])
        es = [jnp.exp(ls - lm) for ls in lses]
        den = es[0] + es[1] + es[2]
        merged = (es[0] * outs[0] + es[1] * outs[1] + es[2] * outs[2]) / den
        o_ref[0, :, h * hd:(h + 1) * hd] = merged[:t_dec, :]


def _dilated_sample_attn(qkv_s, states, layer, slopes):
    bs, t_dec, n_all = qkv_s.shape
    dg = D_GROUP_B
    x8 = jnp.pad(qkv_s, ((0, 0), (0, SUBLANES - t_dec), (0, 0)))
    views, specs = [], []
    for g in range(N_GROUPS_B):
        dil = DILATIONS[g]
        n_l, _, wb = states[g].shape[:3]
        n_back = WINDOWS[g] // dil
        assert wb == WINDOWS[g] == n_back * dil and n_back == LANES
        assert dil & (dil - 1) == 0 and (dil == 1 or dil >= t_dec)
        for r in range(min(dil, t_dec)):
            views.append(states[g][layer, :, r::dil].reshape(bs, n_back, 2 * dg))
            specs.append(pl.BlockSpec((1, n_back, 2 * dg), lambda bi: (bi, 0, 0)))
    return pl.pallas_call(
        functools.partial(_dilated_sample_body, t_dec=t_dec),
        grid=(bs,),
        in_specs=[pl.BlockSpec(memory_space=pltpu.SMEM),
                  pl.BlockSpec((1, SUBLANES, n_all), lambda bi: (bi, 0, 0))] + specs,
        out_specs=pl.BlockSpec((1, t_dec, dg), lambda bi: (bi, 0, 0)),
        out_shape=jax.ShapeDtypeStruct((bs, t_dec, dg), F32),
        compiler_params=_cparams("parallel"),
        name="dilated_sample_attn",
    )(slopes, x8, *views)


def _win_update_body(cur_ref, nxt_ref, new_ref, o_ref):
    wc = cur_ref.shape[1]
    t_dec = nxt_ref.shape[1]
    o_ref[0, 0:wc - t_dec] = cur_ref[0, t_dec:wc]
    last = pl.program_id(1) == pl.num_programs(1) - 1

    @pl.when(last)
    def _():
        o_ref[0, wc - t_dec:wc] = new_ref[0]

    @pl.when(jnp.logical_not(last))
    def _():
        o_ref[0, wc - t_dec:wc] = nxt_ref[0]


def _win_update(state, new):
    n_l, bs, wb = state.shape[:3]
    t_dec = new.shape[2]
    tail = state.shape[3:]
    wc = min(wb, 512)
    assert wb % wc == 0 and wc % t_dec == 0 and wc > t_dec
    n_chunks = wb // wc
    per_chunk = wc // t_dec
    zeros = (0,) * len(tail)
    flat = state.reshape((n_l * bs, wb) + tail)
    out = pl.pallas_call(
        _win_update_body,
        grid=(n_l * bs, n_chunks),
        in_specs=[pl.BlockSpec((1, wc) + tail, lambda r, c: (r, c) + zeros),
                  pl.BlockSpec((1, t_dec) + tail,
                               lambda r, c: (r, jnp.minimum(c + 1, n_chunks - 1) * per_chunk) + zeros),
                  pl.BlockSpec((1, t_dec) + tail, lambda r, c: (r, 0) + zeros)],
        out_specs=pl.BlockSpec((1, wc) + tail, lambda r, c: (r, c) + zeros),
        out_shape=jax.ShapeDtypeStruct(flat.shape, state.dtype),
        compiler_params=_cparams("parallel", "arbitrary"),
        name="win_state_update",
    )(flat, flat, new.reshape((n_l * bs, t_dec) + tail))
    return out.reshape(state.shape)


def _oproj_body(o_ref, w_ref, x_ref, g_ref, out_ref):
    y = jnp.dot(o_ref[0].astype(BF16), w_ref[...], preferred_element_type=F32)
    out_ref[0] = x_ref[0] + g_ref[0] * y


def _oproj_merge_body(o0_ref, o1_ref, o2_ref, l0_ref, l1_ref, l2_ref, w_ref, x_ref, g_ref, out_ref):
    l0, l1, l2 = l0_ref[0], l1_ref[0], l2_ref[0]
    lm = jnp.maximum(jnp.maximum(l0, l1), l2)
    e0, e1, e2 = jnp.exp(l0 - lm), jnp.exp(l1 - lm), jnp.exp(l2 - lm)
    o = (e0 * o0_ref[0] + e1 * o1_ref[0] + e2 * o2_ref[0]) / (e0 + e1 + e2)
    y = jnp.dot(o.astype(BF16), w_ref[...], preferred_element_type=F32)
    out_ref[0] = x_ref[0] + g_ref[0] * y


def _oproj(o_list, w, x, gate):
    b, t, d = x.shape
    din = w.shape[0]
    tm = _row_tile(t)
    tmod = gate.shape[1]
    tmb = tm if tmod == t else 1
    gate_map = (lambda bi, mi: (bi, mi, 0)) if tmod == t else (lambda bi, mi: (bi, 0, 0))
    row = lambda bi, mi: (bi, mi, 0)
    body = _oproj_body if len(o_list) == 1 else _oproj_merge_body
    return pl.pallas_call(
        body,
        grid=(b, t // tm),
        in_specs=[pl.BlockSpec((1, tm, din), row)] * len(o_list) + [
            pl.BlockSpec((din, d), lambda bi, mi: (0, 0)),
            pl.BlockSpec((1, tm, d), row),
            pl.BlockSpec((1, tmb, d), gate_map)],
        out_specs=pl.BlockSpec((1, tm, d), row),
        out_shape=jax.ShapeDtypeStruct((b, t, d), F32),
        compiler_params=_cparams("parallel", "parallel"),
        name="out_proj_residual",
    )(*o_list, w, x, gate)


def _conv_ffn_body(x_ref, sh_ref, sc_ref, gt_ref, g_ref, win_ref, cw_ref, cb_ref, wd_ref, p1_ref, p2_ref,
                   out_ref, a_ref, z_s, carry_s, *, chunk, seq_len, carry_rows):
    tm = x_ref.shape[1]
    dff = wd_ref.shape[0]
    x = x_ref[0]
    h = _norm_mod(x, g_ref[...], sh_ref[0], sc_ref[0]).astype(BF16)
    row = lax.broadcasted_iota(jnp.int32, (tm, chunk), 0)
    t_in_seq = row % seq_len
    if carry_rows:
        @pl.when(pl.program_id(1) == 0)
        def _():
            carry_s[...] = jnp.zeros(carry_s.shape, F32)

    for c in range(dff // chunk):
        cs = slice(c * chunk, (c + 1) * chunk)
        a = jnp.dot(h, win_ref[:, cs], preferred_element_type=F32)
        bgate = jnp.dot(h, win_ref[:, dff + c * chunk:dff + (c + 1) * chunk], preferred_element_type=F32)
        a1 = jnp.where(t_in_seq >= 1, pltpu.roll(a, 1, axis=0), 0.0)
        a2 = jnp.where(t_in_seq >= 2, pltpu.roll(a, 2, axis=0), 0.0)
        if carry_rows:
            prev = carry_s[:, cs]
            a1 = jnp.where(row == 0, prev[SUBLANES - 1:SUBLANES, :], a1)
            a2 = jnp.where(row == 0, prev[SUBLANES - 2:SUBLANES - 1, :],
                           jnp.where(row == 1, prev[SUBLANES - 1:SUBLANES, :], a2))
            carry_s[:, cs] = a[tm - SUBLANES:, :]
            a_ref[0, :, cs] = a[tm - SUBLANES:, :]
        else:
            a1 = a1 + p1_ref[0, :, cs]
            a2 = a2 + p2_ref[0, :, cs]
            a_ref[0, :, cs] = a
        y = cb_ref[:, cs] + cw_ref[0:1, cs] * a2 + cw_ref[1:2, cs] * a1 + cw_ref[2:3, cs] * a
        z_s[:, cs] = (y * jax.nn.sigmoid(y) * bgate).astype(BF16)
    f = jnp.dot(z_s[...], wd_ref[...], preferred_element_type=F32)
    out_ref[0] = x + gt_ref[0] * f


def _conv_ffn(x, sh, sc, gate, g, w_in, conv_w, conv_b, w_down, seq_len, p1=None, p2=None):
    b, t, d = x.shape
    dff = w_down.shape[0]
    tm = _row_tile(t)
    carry_rows = p1 is None
    chunk = 256
    assert dff % chunk == 0 and tm % SUBLANES == 0
    tmod = sh.shape[1]
    tmb = tm if tmod == t else 1
    mod_map = (lambda bi, mi: (bi, mi, 0)) if tmod == t else (lambda bi, mi: (bi, 0, 0))
    row = lambda bi, mi: (bi, mi, 0)
    const = lambda bi, mi: (0, 0)
    if carry_rows:
        assert seq_len == t
        p1 = p2 = jnp.zeros((1, SUBLANES, LANES), F32)
        p_spec = pl.BlockSpec((1, SUBLANES, LANES), lambda bi, mi: (0, 0, 0))
        a_rows = SUBLANES
        a_spec = pl.BlockSpec((1, SUBLANES, dff), lambda bi, mi: (bi, 0, 0))
    else:
        assert tm % seq_len == 0
        p_spec = pl.BlockSpec((1, tm, dff), row)
        a_rows = t
        a_spec = pl.BlockSpec((1, tm, dff), row)
    return pl.pallas_call(
        functools.partial(_conv_ffn_body, chunk=chunk, seq_len=seq_len, carry_rows=carry_rows),
        grid=(b, t // tm),
        in_specs=[pl.BlockSpec((1, tm, d), row),
                  pl.BlockSpec((1, tmb, d), mod_map),
                  pl.BlockSpec((1, tmb, d), mod_map),
                  pl.BlockSpec((1, tmb, d), mod_map),
                  pl.BlockSpec((1, d), const),
                  pl.BlockSpec((d, 2 * dff), const),
                  pl.BlockSpec((CONV_W, dff), const),
                  pl.BlockSpec((1, dff), const),
                  pl.BlockSpec((dff, d), const),
                  p_spec, p_spec],
        out_specs=[pl.BlockSpec((1, tm, d), row), a_spec],
        out_shape=[jax.ShapeDtypeStruct((b, t, d), F32),
                   jax.ShapeDtypeStruct((b, a_rows, dff), F32)],
        scratch_shapes=[pltpu.VMEM((tm, dff), BF16),
                        pltpu.VMEM((SUBLANES, dff), F32)],
        compiler_params=_cparams("parallel", "arbitrary"),
        name="conv_ffn",
    )(x, sh, sc, gate, g, w_in, conv_w, conv_b, w_down, p1, p2)


def _final_norm_body(x_ref, g_ref, o_ref):
    x = x_ref[0]
    ms = jnp.mean(x * x, axis=-1, keepdims=True)
    o_ref[0] = x * lax.rsqrt(ms + RMS_EPS) * g_ref[...]


def _final_norm(x, g):
    b, t, d = x.shape
    tm = _row_tile(t)
    return pl.pallas_call(
        _final_norm_body,
        grid=(b, t // tm),
        in_specs=[pl.BlockSpec((1, tm, d), lambda bi, mi: (bi, mi, 0)),
                  pl.BlockSpec((1, d), lambda bi, mi: (0, 0))],
        out_specs=pl.BlockSpec((1, tm, d), lambda bi, mi: (bi, mi, 0)),
        out_shape=jax.ShapeDtypeStruct((b, t, d), F32),
        compiler_params=_cparams("parallel", "parallel"),
        name="final_norm",
    )(x, g)


def _alibi_slopes(n_heads):
    return jnp.exp2(-8.0 * jnp.arange(1, n_heads + 1, dtype=F32) / n_heads)


def kernel(x_prompt, x_sample, cache_moba_k, cache_moba_v, page_table, state_win1, state_win2, state_win3,
           state_conv, c_prompt, c_sample, ada_w, ada_b, norm1_g, norm2_g, final_g, a_w_qkv, a_w_o,
           b_w_qkv, b_w_o, ffn_w_in, ffn_conv_w, ffn_conv_b, ffn_w_down):
    depth = ada_w.shape[0]
    bp, tp, d = x_prompt.shape
    bs, ts, _ = x_sample.shape
    n_heads_a = d // HEAD_DIM_A
    dff = ffn_w_down.shape[1]
    states = (state_win1, state_win2, state_win3)

    slopes_a = _alibi_slopes(n_heads_a)
    slopes_b = _alibi_slopes(N_GROUPS_B * HEADS_PER_GROUP_B)
    row = jnp.arange(ts * n_heads_a)
    slope_rows = jnp.broadcast_to(slopes_a[row % n_heads_a][:, None], (ts * n_heads_a, MOBA_BLOCK)).astype(F32)
    tq_rows = jnp.broadcast_to((row // n_heads_a)[:, None], (ts * n_heads_a, MOBA_BLOCK)).astype(F32)

    mods = _ada_all(jnp.concatenate([c_prompt, c_sample], axis=0), ada_w, ada_b)
    mods = mods.reshape(depth, bp + bs, 6, d)

    cache_kt = jnp.transpose(cache_moba_k, (0, 1, 3, 4, 2))
    cache_vt = jnp.transpose(cache_moba_v, (0, 1, 3, 4, 2))

    xp = x_prompt
    xs = x_sample.reshape(1, bs * ts, d)
    ka_p, va_p, ka_s, va_s = [], [], [], []
    win_p = [[] for _ in range(N_GROUPS_B)]
    win_new = [[] for _ in range(N_GROUPS_B)]
    conv_p, conv_s = [], []
    for i in range(depth):
        mp = [mods[i, :bp, k].reshape(bp, 1, d) for k in range(6)]
        ms = [jnp.repeat(mods[i, bp:, k], ts, axis=0).reshape(1, bs * ts, d) for k in range(6)]
        g1 = norm1_g[i].reshape(1, d)
        g2 = norm2_g[i].reshape(1, d)
        j = i // 2
        if i % 2 == 0:
            w_o = a_w_o[j].astype(BF16)
            q_p, kb_p, kt_p, vt_p, vtb_p, km_p = _moba_qkv_prompt(xp, mp[0], mp[1], g1, a_w_qkv[j])
            qkv_s = _nm_matmul(xs, ms[0], ms[1], g1, a_w_qkv[j].astype(BF16)).reshape(bs, ts, 3 * d)
            o_p = _moba_prompt_attn(q_p, kb_p, vtb_p, km_p, slopes_a)
            o_s = _moba_sample_attn(qkv_s, cache_kt, cache_vt, j, page_table, slope_rows, tq_rows)
            xp = _oproj([o_p], w_o, xp, mp[2])
            xs = _oproj([o_s.reshape(1, bs * ts, d)], w_o, xs, ms[2])
            ka_p.append(kt_p.reshape(bp, n_heads_a, HEAD_DIM_A, tp))
            va_p.append(vt_p.reshape(bp, n_heads_a, HEAD_DIM_A, tp))
            ka_s.append(qkv_s[:, :, d:2 * d].reshape(bs, ts, n_heads_a, HEAD_DIM_A))
            va_s.append(qkv_s[:, :, 2 * d:].reshape(bs, ts, n_heads_a, HEAD_DIM_A))
        else:
            w_o = b_w_o[j].astype(BF16)
            views_p = _dilated_qkv_prompt(xp, mp[0], mp[1], g1, b_w_qkv[j])
            qkv_s = _nm_matmul(xs, ms[0], ms[1], g1, b_w_qkv[j].astype(BF16)).reshape(bs, ts, -1)
            parts = [_dilated_prompt_group(views_p[g], slopes_b, g) for g in range(N_GROUPS_B)]
            xp = _oproj([p[0] for p in parts] + [p[1] for p in parts], w_o, xp, mp[2])
            o_s = _dilated_sample_attn(qkv_s, states, j, slopes_b)
            xs = _oproj([o_s.reshape(1, bs * ts, D_GROUP_B)], w_o, xs, ms[2])
            kv_s = qkv_s.reshape(bs, ts, 3, N_GROUPS_B, HEADS_PER_GROUP_B, HEAD_DIM_B)[:, :, 1:]
            for g in range(N_GROUPS_B):
                wp = min(WINDOWS[g], tp)
                dil = DILATIONS[g]
                assert wp % dil == 0
                tail = views_p[g][:, (tp - wp) // dil:].reshape(
                    bp, wp, 3, HEADS_PER_GROUP_B, HEAD_DIM_B)
                win_p[g].append(tail[:, :, 1:])
                win_new[g].append(kv_s[:, :, :, g])
        w_in = ffn_w_in[i].astype(BF16)
        w_down = ffn_w_down[i].astype(BF16)
        cw = ffn_conv_w[i]
        cb = ffn_conv_b[i].reshape(1, dff)
        xp, a_tail = _conv_ffn(xp, mp[3], mp[4], mp[5], g2, w_in, cw, cb, w_down, tp)
        conv_p.append(a_tail[:, -(CONV_W - 1):])
        st = state_conv[i]
        zero = jnp.zeros((bs, 1, dff), F32)
        p1 = jnp.concatenate([st[:, 1:2], jnp.tile(zero, (1, ts - 1, 1))], axis=1).reshape(1, bs * ts, dff)
        p2 = jnp.concatenate([st[:, 0:1], st[:, 1:2], jnp.tile(zero, (1, ts - 2, 1))], axis=1).reshape(1, bs * ts, dff)
        xs, a_s = _conv_ffn(xs, ms[3], ms[4], ms[5], g2, w_in, cw, cb, w_down, ts, p1, p2)
        a_ext = jnp.concatenate([st, a_s.reshape(bs, ts, dff)], axis=1)
        conv_s.append(a_ext[:, -(CONV_W - 1):])
    y_prompt = _final_norm(xp, final_g.reshape(1, d))
    y_sample = _final_norm(xs, final_g.reshape(1, d)).reshape(bs, ts, d)
    win_s = [_win_update(states[g], jnp.stack(win_new[g])) for g in range(N_GROUPS_B)]
    moba_k_prompt = jnp.transpose(jnp.stack(ka_p), (0, 1, 4, 2, 3))
    moba_v_prompt = jnp.transpose(jnp.stack(va_p), (0, 1, 4, 2, 3))
    return (y_prompt, y_sample, moba_k_prompt, moba_v_prompt, jnp.stack(ka_s), jnp.stack(va_s),
            jnp.stack(win_p[0]), jnp.stack(win_p[1]), jnp.stack(win_p[2]),
            win_s[0], win_s[1], win_s[2],
            jnp.stack(conv_p), jnp.stack(conv_s))
```

```python
import functools

import jax
import jax.numpy as jnp
from jax import lax
from jax.experimental import pallas as pl
from jax.experimental.pallas import tpu as pltpu

F32 = jnp.float32
BF16 = jnp.bfloat16

HEAD_DIM_A = 64
MOBA_BLOCK = 256
MOBA_TOPK = 3
HEAD_DIM_B = 128
N_GROUPS_B = 3
HEADS_PER_GROUP_B = 4
D_GROUP_B = HEADS_PER_GROUP_B * HEAD_DIM_B
WINDOWS = (128, 512, 2048)
DILATIONS = (1, 4, 16)
CONV_W = 3
RMS_EPS = 1e-6
NEG_INF = -1e30
POS_BIG = 1e30
LOG2E = 1.4426950408889634

LANES = 128
SUBLANES = 8
VMEM_LIMIT = 56 * 1024 * 1024

MOBA_HEADS_PER_STEP = 8
MOBA_BLOCKS_PER_ITER = 2
MOBA_PAGES_PER_STEP = 8

_NT = (((1,), (1,)), ((), ()))


def _cparams(*sem):
    return pltpu.CompilerParams(dimension_semantics=sem, vmem_limit_bytes=VMEM_LIMIT)


def _row_tile(t, cap=512):
    tm = min(t, cap)
    assert t % tm == 0
    return tm


def _norm_mod(x, g, sh, sc):
    ms = jnp.mean(x * x, axis=-1, keepdims=True)
    y = x * lax.rsqrt(ms + RMS_EPS) * g
    return y * (1.0 + sc) + sh


def _ada_body(c_ref, w_ref, b_ref, o_ref):
    c = c_ref[...]
    s = (c * jax.nn.sigmoid(c)).astype(BF16)
    o_ref[0] = jnp.dot(s, w_ref[0].astype(BF16), preferred_element_type=F32) + b_ref[0]


def _ada_all(c_all, ada_w, ada_b):
    depth, d, n6 = ada_w.shape
    nb = c_all.shape[0]
    tn = n6 // 4
    return pl.pallas_call(
        _ada_body,
        grid=(depth, n6 // tn),
        in_specs=[pl.BlockSpec((nb, d), lambda l, n: (0, 0)),
                  pl.BlockSpec((1, d, tn), lambda l, n: (l, 0, n)),
                  pl.BlockSpec((1, 1, tn), lambda l, n: (l, 0, n))],
        out_specs=pl.BlockSpec((1, nb, tn), lambda l, n: (l, 0, n)),
        out_shape=jax.ShapeDtypeStruct((depth, nb, n6), F32),
        compiler_params=_cparams("parallel", "parallel"),
        name="ada_table",
    )(c_all, ada_w, ada_b.reshape(depth, 1, n6))


def _nm_matmul_body(x_ref, sh_ref, sc_ref, g_ref, w_ref, o_ref, h_s):
    @pl.when(pl.program_id(2) == 0)
    def _():
        h_s[...] = _norm_mod(x_ref[0], g_ref[...], sh_ref[0], sc_ref[0]).astype(BF16)

    o_ref[0] = jnp.dot(h_s[...], w_ref[...], preferred_element_type=F32)


def _nm_matmul(x, sh, sc, g, w, n_tiles=3):
    b, t, d = x.shape
    n = w.shape[1]
    tm = _row_tile(t)
    tmod = sh.shape[1]
    tmb = tm if tmod == t else 1
    mod_map = (lambda bi, mi, ni: (bi, mi, 0)) if tmod == t else (lambda bi, mi, ni: (bi, 0, 0))
    tn = n // n_tiles
    return pl.pallas_call(
        _nm_matmul_body,
        grid=(b, t // tm, n_tiles),
        in_specs=[pl.BlockSpec((1, tm, d), lambda bi, mi, ni: (bi, mi, 0)),
                  pl.BlockSpec((1, tmb, d), mod_map),
                  pl.BlockSpec((1, tmb, d), mod_map),
                  pl.BlockSpec((1, d), lambda bi, mi, ni: (0, 0)),
                  pl.BlockSpec((d, tn), lambda bi, mi, ni: (0, ni))],
        out_specs=pl.BlockSpec((1, tm, tn), lambda bi, mi, ni: (bi, mi, ni)),
        out_shape=jax.ShapeDtypeStruct((b, t, n), F32),
        scratch_shapes=[pltpu.VMEM((tm, d), BF16)],
        compiler_params=_cparams("parallel", "parallel", "arbitrary"),
        name="norm_mod_matmul",
    )(x, sh, sc, g, w)


def _moba_qkv_body(x_ref, sh_ref, sc_ref, g_ref, wq_ref, wk_ref, wkt_ref, wvt_ref,
                   q_ref, kb_ref, kt_ref, vt_ref, vtb_ref, km_ref):
    tm = x_ref.shape[1]
    per_tile = tm // MOBA_BLOCK
    h = _norm_mod(x_ref[0], g_ref[...], sh_ref[0], sc_ref[0]).astype(BF16)
    q_ref[0] = jnp.dot(h, wq_ref[...], preferred_element_type=F32)
    k = jnp.dot(h, wk_ref[...], preferred_element_type=F32)
    kb_ref[0] = k.astype(BF16)
    first = pl.program_id(1) * per_tile
    for j in range(per_tile):
        km_ref[0, pl.ds(first + j, 1), :] = (
            jnp.sum(k[j * MOBA_BLOCK:(j + 1) * MOBA_BLOCK], axis=0, keepdims=True) * (1.0 / MOBA_BLOCK))
    kt_ref[0] = lax.dot_general(wkt_ref[...], h, _NT, preferred_element_type=F32)
    vt = lax.dot_general(wvt_ref[...], h, _NT, preferred_element_type=F32)
    vt_ref[0] = vt
    for j in range(per_tile):
        vtb_ref[0, j] = vt[:, j * MOBA_BLOCK:(j + 1) * MOBA_BLOCK].astype(BF16)


def _moba_qkv_prompt(x, sh, sc, g, w_qkv):
    b, t, d = x.shape
    tm = _row_tile(t)
    assert tm % MOBA_BLOCK == 0
    nb = t // MOBA_BLOCK
    wq = w_qkv[:, :d].astype(BF16)
    wk = w_qkv[:, d:2 * d].astype(BF16)
    wkt = w_qkv[:, d:2 * d].T.astype(BF16)
    wvt = w_qkv[:, 2 * d:].T.astype(BF16)
    row = lambda bi, mi: (bi, mi, 0)
    col = lambda bi, mi: (bi, 0, mi)
    mod = lambda bi, mi: (bi, 0, 0)
    const = lambda bi, mi: (0, 0)
    return pl.pallas_call(
        _moba_qkv_body,
        grid=(b, t // tm),
        in_specs=[pl.BlockSpec((1, tm, d), row),
                  pl.BlockSpec((1, 1, d), mod),
                  pl.BlockSpec((1, 1, d), mod),
                  pl.BlockSpec((1, d), const)] + [pl.BlockSpec((d, d), const)] * 4,
        out_specs=[pl.BlockSpec((1, tm, d), row),
                   pl.BlockSpec((1, tm, d), row),
                   pl.BlockSpec((1, d, tm), col),
                   pl.BlockSpec((1, d, tm), col),
                   pl.BlockSpec((1, tm // MOBA_BLOCK, d, MOBA_BLOCK), lambda bi, mi: (bi, mi, 0, 0)),
                   pl.BlockSpec((1, nb, d), mod)],
        out_shape=[jax.ShapeDtypeStruct((b, t, d), F32),
                   jax.ShapeDtypeStruct((b, t, d), BF16),
                   jax.ShapeDtypeStruct((b, d, t), F32),
                   jax.ShapeDtypeStruct((b, d, t), F32),
                   jax.ShapeDtypeStruct((b, nb, d, MOBA_BLOCK), BF16),
                   jax.ShapeDtypeStruct((b, nb, d), F32)],
        compiler_params=_cparams("parallel", "arbitrary"),
        name="moba_qkv_prompt",
    )(x, sh, sc, g, wq, wk, wkt, wvt)


def _topk_rank_select(gate, blk, n_valid, n_rows):
    gate = jnp.where(blk < n_valid, gate, NEG_INF)
    rank = jnp.zeros(gate.shape, F32)
    for m in range(n_rows):
        gm = gate[m:m + 1, :]
        beats = (gm > gate) | ((gm == gate) & (blk > m))
        rank = rank + jnp.where(beats, 1.0, 0.0)
    n_sel = jnp.minimum(n_valid, MOBA_TOPK).astype(F32)
    return jnp.where((rank < n_sel) & (blk < n_valid), 1.0, 0.0)


def _moba_prompt_body(slopes_ref, q_ref, kb_ref, vtb_ref, km_ref, o_ref,
                      d0_s, sel_s, qb_s, m_s, l_s, acc_s, s_s, *, nb, hps, per_iter):
    blk_sz, hd = MOBA_BLOCK, HEAD_DIM_A
    head0 = pl.program_id(1) * hps
    i = pl.program_id(2)

    @pl.when(i == 0)
    def _init():
        rk = lax.broadcasted_iota(jnp.int32, (blk_sz, blk_sz), 0)
        rq = lax.broadcasted_iota(jnp.int32, (blk_sz, blk_sz), 1)
        rel = (rk - rq).astype(F32)
        for h in range(hps):
            d0 = (slopes_ref[head0 + h] * LOG2E) * rel
            d0_s[h, 0] = d0
            d0_s[h, 1] = jnp.where(rk <= rq, d0, NEG_INF)

    lane = lax.broadcasted_iota(jnp.int32, (1, LANES), 1)
    blk = lax.broadcasted_iota(jnp.int32, (nb, blk_sz), 0)
    for h in range(hps):
        pair, hh = divmod(h, LANES // hd)
        cs = slice(pair * LANES, (pair + 1) * LANES)
        qh = jnp.where((lane >= hh * hd) & (lane < (hh + 1) * hd), q_ref[0, :, cs] * (hd ** -0.5 * LOG2E), 0.0)
        q_hi = qh.astype(BF16)
        q_lo = (qh - q_hi.astype(F32)).astype(BF16)
        km = km_ref[0, :, cs]
        km_hi = km.astype(BF16)
        km_lo = (km - km_hi.astype(F32)).astype(BF16)
        g2 = lax.dot_general(jnp.concatenate([km_hi, km_lo], axis=0), q_hi, _NT, preferred_element_type=F32)
        gate = g2[:nb] + g2[nb:] + lax.dot_general(km_hi, q_lo, _NT, preferred_element_type=F32)
        sel_s[h] = jnp.where(blk == i, 1.0, _topk_rank_select(gate, blk, i, nb))
        qb_s[h] = q_hi
        m_s[h] = jnp.full((1, blk_sz), NEG_INF, F32)
        l_s[h] = jnp.zeros((1, blk_sz), F32)
        acc_s[h] = jnp.zeros((hd, blk_sz), F32)

    def key_blocks(it, carry):
        blocks = []
        for u in range(per_iter):
            n = it * per_iter + u
            n_mem = jnp.minimum(n, nb - 1)
            rows = pl.ds(pl.multiple_of(n_mem * blk_sz, blk_sz), blk_sz)
            own = (n == i).astype(jnp.int32)
            for h in range(hps):
                pair = h // (LANES // hd)
                kblk = kb_ref[0, rows, pair * LANES:(pair + 1) * LANES]
                s_s[u, h] = (lax.dot_general(kblk, qb_s[h], _NT, preferred_element_type=F32)
                             + d0_s[h, own])
            blocks.append((n, n_mem))
        for u, (n, n_mem) in enumerate(blocks):
            off = ((i - n) * blk_sz).astype(F32)
            for h in range(hps):
                c_n = -(slopes_ref[head0 + h] * LOG2E) * off
                selrow = (sel_s[h, pl.ds(n_mem, 1), :] > 0.5) & (n <= i)
                m_old = m_s[h]
                cmax = jnp.max(s_s[u, h], axis=0, keepdims=True)
                m_new = jnp.maximum(m_old, jnp.where(selrow, cmax + c_n, NEG_INF))
                alpha = jnp.exp2(m_old - m_new)
                p = jnp.exp2(s_s[u, h] - jnp.where(selrow, m_new - c_n, POS_BIG))
                l_s[h] = alpha * l_s[h] + jnp.sum(p, axis=0, keepdims=True)
                vt = vtb_ref[0, n_mem, h * hd:(h + 1) * hd, :]
                acc_s[h] = alpha * acc_s[h] + jnp.dot(vt, p.astype(BF16), preferred_element_type=F32)
                m_s[h] = m_new
        return carry

    lax.fori_loop(0, (i + per_iter) // per_iter, key_blocks, 0)
    per_pair = LANES // hd
    for pair in range(hps // per_pair):
        outs = [acc_s[pair * per_pair + hh] / l_s[pair * per_pair + hh] for hh in range(per_pair)]
        o_ref[0, :, pair * LANES:(pair + 1) * LANES] = jnp.concatenate(outs, axis=0).T.astype(o_ref.dtype)


def _moba_prompt_attn(q, kb, vtb, km, slopes):
    b, t, d = q.shape
    nb = t // MOBA_BLOCK
    hps = MOBA_HEADS_PER_STEP
    w = hps * HEAD_DIM_A
    assert t % MOBA_BLOCK == 0 and d % w == 0 and w % LANES == 0
    return pl.pallas_call(
        functools.partial(_moba_prompt_body, nb=nb, hps=hps, per_iter=MOBA_BLOCKS_PER_ITER),
        grid=(b, d // w, nb),
        in_specs=[pl.BlockSpec(memory_space=pltpu.SMEM),
                  pl.BlockSpec((1, MOBA_BLOCK, w), lambda bi, hg, i: (bi, i, hg)),
                  pl.BlockSpec((1, t, w), lambda bi, hg, i: (bi, 0, hg)),
                  pl.BlockSpec((1, nb, w, MOBA_BLOCK), lambda bi, hg, i: (bi, 0, hg, 0)),
                  pl.BlockSpec((1, nb, w), lambda bi, hg, i: (bi, 0, hg))],
        out_specs=pl.BlockSpec((1, MOBA_BLOCK, w), lambda bi, hg, i: (bi, i, hg)),
        out_shape=jax.ShapeDtypeStruct((b, t, d), BF16),
        scratch_shapes=[pltpu.VMEM((hps, 2, MOBA_BLOCK, MOBA_BLOCK), F32),
                        pltpu.VMEM((hps, nb, MOBA_BLOCK), F32),
                        pltpu.VMEM((hps, MOBA_BLOCK, LANES), BF16),
                        pltpu.VMEM((hps, 1, MOBA_BLOCK), F32),
                        pltpu.VMEM((hps, 1, MOBA_BLOCK), F32),
                        pltpu.VMEM((hps, HEAD_DIM_A, MOBA_BLOCK), F32),
                        pltpu.VMEM((MOBA_BLOCKS_PER_ITER, hps, MOBA_BLOCK, MOBA_BLOCK), F32)],
        compiler_params=_cparams("parallel", "parallel", "arbitrary"),
        name="moba_prompt_attn",
    )(slopes, q, kb, vtb, km)


def _head_block_diag(q, n_heads, hd):
    d = n_heads * hd
    n_t = LANES // n_heads
    row_h = lax.broadcasted_iota(jnp.int32, (n_heads, d), 0)
    col_h = lax.broadcasted_iota(jnp.int32, (n_heads, d), 1) // hd
    pieces = []
    for t in range(n_t):
        if t < q.shape[0]:
            pieces.append(jnp.where(row_h == col_h, jnp.broadcast_to(q[t:t + 1, :], (n_heads, d)), 0.0))
        else:
            pieces.append(jnp.zeros((n_heads, d), F32))
    return jnp.concatenate(pieces, axis=0)


def _moba_s_scores_body(pt_ref, q_ref, *refs, n_heads, t_dec, pages):
    k_refs, s_ref, qs_s = refs[:pages], refs[pages], refs[pages + 1]
    hd = HEAD_DIM_A
    d = n_heads * hd
    rows = t_dec * n_heads
    page = k_refs[0].shape[-1]
    per_blk = MOBA_BLOCK // page

    @pl.when(pl.program_id(1) == 0)
    def _():
        qbd = _head_block_diag(q_ref[0] * (hd ** -0.5), n_heads, hd)[:rows]
        hi = qbd.astype(BF16)
        qs_s[0:rows] = hi
        qs_s[rows:2 * rows] = (qbd - hi.astype(F32)).astype(BF16)

    for p, k_ref in enumerate(k_refs):
        lanes = slice((p % per_blk) * page, (p % per_blk + 1) * page)
        kp = k_ref[0, 0].reshape(d, page)
        k_hi = kp.astype(BF16)
        k_lo = (kp - k_hi.astype(F32)).astype(BF16)
        a = jnp.dot(qs_s[...], k_hi, preferred_element_type=F32)
        b = jnp.dot(qs_s[0:rows], k_lo, preferred_element_type=F32)
        s_ref[0, p // per_blk, :, lanes] = a[:rows] + a[rows:] + b


def _moba_s_softmax_body(s_ref, q_ref, kn_ref, sl_ref, tq_ref, p_ref, po_ref, l_ref,
                         *, n_heads, n_blocks, t_dec, past_len):
    hd = HEAD_DIM_A
    rows = t_dec * n_heads
    blk = MOBA_BLOCK
    lane = lax.broadcasted_iota(jnp.int32, (rows, LANES), 1)
    gate = jnp.full((rows, LANES), NEG_INF, F32)
    for n in range(n_blocks):
        gate = jnp.where(lane == n, jnp.sum(s_ref[0, n], axis=1, keepdims=True), gate)
    rank = jnp.zeros((rows, LANES), F32)
    for m in range(n_blocks):
        gm = gate[:, m:m + 1]
        beats = (gm > gate) | ((gm == gate) & (lane > m))
        rank = rank + jnp.where(beats, 1.0, 0.0)
    sel = (rank < min(MOBA_TOPK, n_blocks)) & (lane < n_blocks)

    sl = sl_ref[...]
    tq = tq_ref[...]
    tok = lax.broadcasted_iota(jnp.int32, (rows, blk), 1).astype(F32)
    dist0 = past_len + tq - tok
    sl1, tq1 = sl[:, 0:1], tq[:, 0:1]

    qbd = _head_block_diag(q_ref[0] * (hd ** -0.5), n_heads, hd)[:rows]
    kn = kn_ref[0]
    own = []
    for tk in range(t_dec):
        s = jnp.sum(qbd * kn[tk:tk + 1, :], axis=1, keepdims=True)
        dd = tq1 - tk
        own.append(jnp.where(dd >= 0, s - sl1 * dd, NEG_INF))
    m = own[0]
    for s in own[1:]:
        m = jnp.maximum(m, s)
    for n in range(n_blocks):
        sb = s_ref[0, n] - sl * (dist0 - n * blk)
        m = jnp.maximum(m, jnp.where(sel[:, n:n + 1], jnp.max(sb, axis=1, keepdims=True), NEG_INF))
    l = jnp.zeros((rows, 1), F32)
    for n in range(n_blocks):
        sb = s_ref[0, n] - sl * (dist0 - n * blk)
        p = jnp.exp(sb - jnp.where(sel[:, n:n + 1], m, POS_BIG))
        l = l + jnp.sum(p, axis=1, keepdims=True)
        p_ref[0, n, 0:rows, :] = p.astype(BF16)
        p_ref[0, n, rows:, :] = jnp.zeros((p_ref.shape[2] - rows, blk), BF16)
    po = jnp.zeros((rows, LANES), F32)
    for tk in range(t_dec):
        pk = jnp.exp(own[tk] - m)
        l = l + pk
        po = jnp.where(lane == tk, pk, po)
    po_ref[0] = po
    l_ref[0] = jnp.broadcast_to(l, (rows, LANES))


def _moba_s_pv_body(pt_ref, p_ref, po_ref, l_ref, vn_ref, *refs, n_heads, t_dec, pages):
    v_refs, o_ref, acc_s = refs[:pages], refs[pages], refs[pages + 1]
    hd = HEAD_DIM_A
    n = pl.program_id(1)
    page = v_refs[0].shape[-1]
    per_blk = MOBA_BLOCK // page
    d = n_heads * hd
    rows = t_dec * n_heads

    @pl.when(n == 0)
    def _():
        acc_s[...] = jnp.zeros(acc_s.shape, F32)

    tot = None
    for p, v_ref in enumerate(v_refs):
        vt = v_ref[0, 0].reshape(d, page).astype(BF16)
        pp = p_ref[0, p // per_blk][:, (p % per_blk) * page:(p % per_blk + 1) * page]
        y = lax.dot_general(vt, pp, _NT, preferred_element_type=F32)
        tot = y if tot is None else tot + y
    acc_s[...] += tot

    @pl.when(n == pl.num_programs(1) - 1)
    def _():
        acc = acc_s[...].T[:rows]
        po = po_ref[0]
        vn = vn_ref[0]
        for tk in range(t_dec):
            acc = acc + po[:, tk:tk + 1] * vn[tk:tk + 1, :]
        row_h = lax.broadcasted_iota(jnp.int32, (rows, d), 0) % n_heads
        col_h = lax.broadcasted_iota(jnp.int32, (rows, d), 1) // hd
        acc = jnp.where(row_h == col_h, acc / l_ref[0][:, 0:1], 0.0)
        for t in range(t_dec):
            o_ref[0, t:t + 1, :] = jnp.sum(acc[t * n_heads:(t + 1) * n_heads, :], axis=0, keepdims=True)


def _moba_sample_attn(qkv_s, cache_kt, cache_vt, layer, page_table, slope_rows, tq_rows):
    bs, t_dec, d3 = qkv_s.shape
    d = d3 // 3
    n_heads, hd, page = cache_kt.shape[2:]
    n_pages = page_table.shape[1]
    past_len = n_pages * page
    pages = MOBA_PAGES_PER_STEP
    rows = t_dec * n_heads
    assert hd == HEAD_DIM_A and n_heads * hd == d
    assert MOBA_BLOCK % page == 0 and past_len % MOBA_BLOCK == 0 and page == LANES
    assert rows % SUBLANES == 0 and rows <= LANES and t_dec <= SUBLANES
    n_blocks = past_len // MOBA_BLOCK
    blocks_per_step = pages * page // MOBA_BLOCK
    assert MOBA_TOPK <= n_blocks <= LANES and n_pages % pages == 0 and (pages * page) % MOBA_BLOCK == 0
    n_steps = n_pages // pages
    pad = SUBLANES - t_dec
    q8 = jnp.pad(qkv_s[:, :, :d], ((0, 0), (0, pad), (0, 0)))
    k8 = jnp.pad(qkv_s[:, :, d:2 * d], ((0, 0), (0, pad), (0, 0)))
    v8 = jnp.pad(qkv_s[:, :, 2 * d:], ((0, 0), (0, pad), (0, 0)))

    def page_spec(p):
        return pl.BlockSpec((1, 1, n_heads, hd, page),
                            lambda bi, n, pt: (layer, pt[bi, pages * n + p], 0, 0, 0))

    small = lambda bi, n, pt: (bi, 0, 0)
    s_all = pl.pallas_call(
        functools.partial(_moba_s_scores_body, n_heads=n_heads, t_dec=t_dec, pages=pages),
        grid_spec=pltpu.PrefetchScalarGridSpec(
            num_scalar_prefetch=1,
            grid=(bs, n_steps),
            in_specs=[pl.BlockSpec((1, SUBLANES, d), small)] + [page_spec(p) for p in range(pages)],
            out_specs=pl.BlockSpec((1, blocks_per_step, rows, MOBA_BLOCK), lambda bi, n, pt: (bi, n, 0, 0)),
            scratch_shapes=[pltpu.VMEM((2 * rows, d), BF16)]),
        out_shape=jax.ShapeDtypeStruct((bs, n_blocks, rows, MOBA_BLOCK), F32),
        compiler_params=_cparams("parallel", "arbitrary"),
        name="moba_sample_scores",
    )(page_table, q8, *([cache_kt] * pages))

    per_b = lambda bi: (bi, 0, 0)
    const = lambda bi: (0, 0)
    p_all, p_own, l_sum = pl.pallas_call(
        functools.partial(_moba_s_softmax_body, n_heads=n_heads, n_blocks=n_blocks, t_dec=t_dec,
                          past_len=past_len),
        grid=(bs,),
        in_specs=[pl.BlockSpec((1, n_blocks, rows, MOBA_BLOCK), lambda bi: (bi, 0, 0, 0)),
                  pl.BlockSpec((1, SUBLANES, d), per_b),
                  pl.BlockSpec((1, SUBLANES, d), per_b),
                  pl.BlockSpec((rows, MOBA_BLOCK), const),
                  pl.BlockSpec((rows, MOBA_BLOCK), const)],
        out_specs=[pl.BlockSpec((1, n_blocks, LANES, MOBA_BLOCK), lambda bi: (bi, 0, 0, 0)),
                   pl.BlockSpec((1, rows, LANES), per_b),
                   pl.BlockSpec((1, rows, LANES), per_b)],
        out_shape=[jax.ShapeDtypeStruct((bs, n_blocks, LANES, MOBA_BLOCK), BF16),
                   jax.ShapeDtypeStruct((bs, rows, LANES), F32),
                   jax.ShapeDtypeStruct((bs, rows, LANES), F32)],
        compiler_params=_cparams("parallel"),
        name="moba_sample_softmax",
    )(s_all, q8, k8, slope_rows, tq_rows)

    return pl.pallas_call(
        functools.partial(_moba_s_pv_body, n_heads=n_heads, t_dec=t_dec, pages=pages),
        grid_spec=pltpu.PrefetchScalarGridSpec(
            num_scalar_prefetch=1,
            grid=(bs, n_steps),
            in_specs=[pl.BlockSpec((1, blocks_per_step, LANES, MOBA_BLOCK), lambda bi, n, pt: (bi, n, 0, 0)),
                      pl.BlockSpec((1, rows, LANES), small),
                      pl.BlockSpec((1, rows, LANES), small),
                      pl.BlockSpec((1, SUBLANES, d), small)] + [page_spec(p) for p in range(pages)],
            out_specs=pl.BlockSpec((1, t_dec, d), small),
            scratch_shapes=[pltpu.VMEM((d, LANES), F32)]),
        out_shape=jax.ShapeDtypeStruct((bs, t_dec, d), F32),
        compiler_params=_cparams("parallel", "arbitrary"),
        name="moba_sample_pv",
    )(page_table, p_all, p_own, l_sum, v8, *([cache_vt] * pages))


def _dilated_prompt_body(slopes_ref, q_ref, kp_ref, kc_ref, vp_ref, vc_ref, o_ref, lse_ref, bias_s,
                         *, group, dilation, n_keys):
    tq = q_ref.shape[1]
    hd = HEAD_DIM_B
    first = (pl.program_id(0) == 0) & (pl.program_id(1) == 0) & (pl.program_id(2) == 0)

    @pl.when(first)
    def _():
        iq = lax.broadcasted_iota(jnp.int32, (tq, 2 * tq), 0)
        jk = lax.broadcasted_iota(jnp.int32, (tq, 2 * tq), 1)
        steps = iq + tq - jk
        ok = (steps >= 0) & (steps < n_keys)
        dist = (steps * dilation).astype(F32)
        for h in range(HEADS_PER_GROUP_B):
            b = -slopes_ref[group * HEADS_PER_GROUP_B + h] * dist
            bias_s[0, h] = jnp.where(ok & (jk >= tq), b, NEG_INF)
            bias_s[1, h] = jnp.where(ok, b, NEG_INF)

    var = jnp.minimum(pl.program_id(2), 1)
    scale = hd ** -0.5
    def scores(h):
        cs = slice(h * hd, (h + 1) * hd)
        qh = q_ref[0, :, cs].astype(BF16)
        return (lax.dot_general(qh, kp_ref[0, :, cs].astype(BF16), _NT, preferred_element_type=F32),
                lax.dot_general(qh, kc_ref[0, :, cs].astype(BF16), _NT, preferred_element_type=F32))

    s_next = scores(0)
    for h in range(HEADS_PER_GROUP_B):
        cs = slice(h * hd, (h + 1) * hd)
        bias = bias_s[var, h]
        s_p = s_next[0] * scale + bias[:, :tq]
        s_c = s_next[1] * scale + bias[:, tq:]
        if h + 1 < HEADS_PER_GROUP_B:
            s_next = scores(h + 1)
        m = jnp.maximum(jnp.max(s_p, axis=1, keepdims=True), jnp.max(s_c, axis=1, keepdims=True))
        p_p = jnp.exp(s_p - m)
        p_c = jnp.exp(s_c - m)
        l = jnp.sum(p_p, axis=1, keepdims=True) + jnp.sum(p_c, axis=1, keepdims=True)
        o = (jnp.dot(p_p.astype(BF16), vp_ref[0, :, cs].astype(BF16), preferred_element_type=F32)
             + jnp.dot(p_c.astype(BF16), vc_ref[0, :, cs].astype(BF16), preferred_element_type=F32))
        o_ref[0, :, cs] = o / l
        lse_ref[0, :, cs] = jnp.broadcast_to(m + jnp.log(l), (tq, hd))


def _dilated_qkv_body(x_ref, sh_ref, sc_ref, g_ref, w_ref, o0_ref, o1_ref, o2_ref, h_s, r_s):
    grp = pl.program_id(2)
    tm = x_ref.shape[1]

    @pl.when(grp == 0)
    def _():
        h_s[...] = _norm_mod(x_ref[0], g_ref[...], sh_ref[0], sc_ref[0]).astype(BF16)

    res = jnp.dot(h_s[...], w_ref[...], preferred_element_type=F32)
    width = res.shape[1]
    for g, o_ref in enumerate((o0_ref, o1_ref, o2_ref)):
        dil = DILATIONS[g]

        @pl.when(grp == g)
        def _(o_ref=o_ref, dil=dil):
            if dil == 1:
                o_ref[0] = res
            else:
                for c in range(width // LANES):
                    r_s[c] = res[:, c * LANES:(c + 1) * LANES]
                for r in range(dil):
                    for c in range(width // LANES):
                        o_ref[0, :, r * width + c * LANES:r * width + (c + 1) * LANES] = (
                            r_s[c, pl.ds(r, tm // dil, stride=dil), :])


def _dilated_qkv_prompt(x, sh, sc, g, w_qkv):
    b, t, d = x.shape
    dg = D_GROUP_B
    tm = _row_tile(t)
    assert all(tm % dil == 0 and (tm // dil) % SUBLANES == 0 for dil in DILATIONS)
    w = w_qkv.reshape(d, 3, N_GROUPS_B, dg).transpose(0, 2, 1, 3).reshape(d, N_GROUPS_B * 3 * dg).astype(BF16)
    mod = lambda bi, mi, gi: (bi, 0, 0)
    return pl.pallas_call(
        _dilated_qkv_body,
        grid=(b, t // tm, N_GROUPS_B),
        in_specs=[pl.BlockSpec((1, tm, d), lambda bi, mi, gi: (bi, mi, 0)),
                  pl.BlockSpec((1, 1, d), mod),
                  pl.BlockSpec((1, 1, d), mod),
                  pl.BlockSpec((1, d), lambda bi, mi, gi: (0, 0)),
                  pl.BlockSpec((d, 3 * dg), lambda bi, mi, gi: (0, gi))],
        out_specs=[pl.BlockSpec((1, tm // dil, dil * 3 * dg), lambda bi, mi, gi: (bi, mi, 0))
                   for dil in DILATIONS],
        out_shape=[jax.ShapeDtypeStruct((b, t // dil, dil * 3 * dg), F32) for dil in DILATIONS],
        scratch_shapes=[pltpu.VMEM((tm, d), BF16), pltpu.VMEM((3 * dg // LANES, tm, LANES), F32)],
        compiler_params=_cparams("parallel", "arbitrary", "arbitrary"),
        name="dilated_qkv_prompt",
    )(x, sh, sc, g, w)


def _dilated_prompt_group(view, slopes, group):
    dg = D_GROUP_B
    dil = DILATIONS[group]
    b, tr, _ = view.shape
    t = tr * dil
    n_keys = WINDOWS[group] // dil + 1
    tq = n_keys - 1
    assert tr % tq == 0 and tq % LANES == 0

    def sec(section, prev):
        def index(bi, r, ti):
            row = jnp.maximum(ti - 1, 0) if prev else ti
            return (bi, row, r * 3 + section)
        return pl.BlockSpec((1, tq, dg), index)

    out_spec = pl.BlockSpec((1, tq, dg), lambda bi, r, ti: (bi, ti, r))
    o, lse = pl.pallas_call(
        functools.partial(_dilated_prompt_body, group=group, dilation=dil, n_keys=n_keys),
        grid=(b, dil, tr // tq),
        in_specs=[pl.BlockSpec(memory_space=pltpu.SMEM),
                  sec(0, False), sec(1, True), sec(1, False), sec(2, True), sec(2, False)],
        out_specs=[out_spec, out_spec],
        out_shape=[jax.ShapeDtypeStruct((b, tr, dil * dg), F32)] * 2,
        scratch_shapes=[pltpu.VMEM((2, HEADS_PER_GROUP_B, tq, 2 * tq), F32)],
        compiler_params=_cparams("arbitrary", "arbitrary", "arbitrary"),
        name=f"dilated_prompt_g{group}",
    )(slopes, view, view, view, view, view)
    return o.reshape(b, t, dg), lse.reshape(b, t, dg)


def _dilated_sample_body(slopes_ref, x_ref, *refs, t_dec):
    hd = HEAD_DIM_B
    dg = D_GROUP_B
    scale = hd ** -0.5
    x = x_ref[0]
    rows = x.shape[0]
    tq = lax.broadcasted_iota(jnp.int32, (rows, LANES), 0)
    mcol = lax.broadcasted_iota(jnp.int32, (rows, LANES), 1)
    tq1 = lax.broadcasted_iota(jnp.int32, (rows, 1), 0)
    o_ref = refs[-1]
    bufs, pos = [], 0
    for g in range(N_GROUPS_B):
        n_res = min(DILATIONS[g], t_dec)
        bufs.append(refs[pos:pos + n_res])
        pos += n_res
    for h in range(HEADS_PER_GROUP_B):
        outs, lses = [], []
        for g in range(N_GROUPS_B):
            dil = DILATIONS[g]
            n_back = WINDOWS[g] // dil
            slope = slopes_ref[g * HEADS_PER_GROUP_B + h]
            c0 = g * dg + h * hd
            qh = x[:, c0:c0 + hd]
            kn = x[:, N_GROUPS_B * dg + c0:N_GROUPS_B * dg + c0 + hd]
            vn = x[:, 2 * N_GROUPS_B * dg + c0:2 * N_GROUPS_B * dg + c0 + hd]
            qb = qh.astype(BF16)
            s_list, v_list = [], []
            for r in range(min(dil, t_dec)):
                kr = bufs[g][r][0, :, h * hd:(h + 1) * hd]
                v_list.append(bufs[g][r][0, :, dg + h * hd:dg + (h + 1) * hd])
                num = dil * (n_back - mcol) + (tq - r)
                ok = (num >= 0) & (num <= dil * n_back) & ((num & (dil - 1)) == 0)
                s = lax.dot_general(qb, kr.astype(BF16), _NT, preferred_element_type=F32) * scale
                s_list.append(jnp.where(ok, s - slope * num.astype(F32), NEG_INF))
            s_new = []
            for tk in range(t_dec):
                num = tq1 - tk
                ok = (num >= 0) & (num <= dil * n_back) & ((num & (dil - 1)) == 0)
                s = jnp.sum(qh * kn[tk:tk + 1, :], axis=1, keepdims=True) * scale
                s_new.append(jnp.where(ok, s - slope * num.astype(F32), NEG_INF))
            m = s_new[0]
            for s in s_new[1:]:
                m = jnp.maximum(m, s)
            for s in s_list:
                m = jnp.maximum(m, jnp.max(s, axis=1, keepdims=True))
            l = jnp.zeros((rows, 1), F32)
            o = jnp.zeros((rows, hd), F32)
            for s, v in zip(s_list, v_list):
                p = jnp.exp(s - m)
                l = l + jnp.sum(p, axis=1, keepdims=True)
                o = o + jnp.dot(p.astype(BF16), v.astype(BF16), preferred_element_type=F32)
            for tk, s in enumerate(s_new):
                p = jnp.exp(s - m)
                l = l + p
                o = o + p * vn[tk:tk + 1, :]
            outs.append(o / l)
            lses.append(m + jnp.log(l))
        lm = jnp.maximum(jnp.maximum(lses[0], lses[1]), lses[2])
        es = [jnp.exp(ls - lm) for ls in lses]
        den = es[0] + es[1] + es[2]
        merged = (es[0] * outs[0] + es[1] * outs[1] + es[2] * outs[2]) / den
        o_ref[0, :, h * hd:(h + 1) * hd] = merged[:t_dec, :]


def _dilated_sample_attn(qkv_s, states, layer, slopes):
    bs, t_dec, n_all = qkv_s.shape
    dg = D_GROUP_B
    x8 = jnp.pad(qkv_s, ((0, 0), (0, SUBLANES - t_dec), (0, 0)))
    views, specs = [], []
    for g in range(N_GROUPS_B):
        dil = DILATIONS[g]
        n_l, _, wb = states[g].shape[:3]
        n_back = WINDOWS[g] // dil
        assert wb == WINDOWS[g] == n_back * dil and n_back == LANES
        assert dil & (dil - 1) == 0 and (dil == 1 or dil >= t_dec)
        for r in range(min(dil, t_dec)):
            views.append(states[g][layer, :, r::dil].reshape(bs, n_back, 2 * dg))
            specs.append(pl.BlockSpec((1, n_back, 2 * dg), lambda bi: (bi, 0, 0)))
    return pl.pallas_call(
        functools.partial(_dilated_sample_body, t_dec=t_dec),
        grid=(bs,),
        in_specs=[pl.BlockSpec(memory_space=pltpu.SMEM),
                  pl.BlockSpec((1, SUBLANES, n_all), lambda bi: (bi, 0, 0))] + specs,
        out_specs=pl.BlockSpec((1, t_dec, dg), lambda bi: (bi, 0, 0)),
        out_shape=jax.ShapeDtypeStruct((bs, t_dec, dg), F32),
        compiler_params=_cparams("parallel"),
        name="dilated_sample_attn",
    )(slopes, x8, *views)


def _win_update_body(cur_ref, nxt_ref, new_ref, o_ref):
    wc = cur_ref.shape[1]
    t_dec = nxt_ref.shape[1]
    o_ref[0, 0:wc - t_dec] = cur_ref[0, t_dec:wc]
    last = pl.program_id(1) == pl.num_programs(1) - 1

    @pl.when(last)
    def _():
        o_ref[0, wc - t_dec:wc] = new_ref[0]

    @pl.when(jnp.logical_not(last))
    def _():
        o_ref[0, wc - t_dec:wc] = nxt_ref[0]


def _win_update(state, new):
    n_l, bs, wb = state.shape[:3]
    t_dec = new.shape[2]
    tail = state.shape[3:]
    wc = min(wb, 512)
    assert wb % wc == 0 and wc % t_dec == 0 and wc > t_dec
    n_chunks = wb // wc
    per_chunk = wc // t_dec
    zeros = (0,) * len(tail)
    flat = state.reshape((n_l * bs, wb) + tail)
    out = pl.pallas_call(
        _win_update_body,
        grid=(n_l * bs, n_chunks),
        in_specs=[pl.BlockSpec((1, wc) + tail, lambda r, c: (r, c) + zeros),
                  pl.BlockSpec((1, t_dec) + tail,
                               lambda r, c: (r, jnp.minimum(c + 1, n_chunks - 1) * per_chunk) + zeros),
                  pl.BlockSpec((1, t_dec) + tail, lambda r, c: (r, 0) + zeros)],
        out_specs=pl.BlockSpec((1, wc) + tail, lambda r, c: (r, c) + zeros),
        out_shape=jax.ShapeDtypeStruct(flat.shape, state.dtype),
        compiler_params=_cparams("parallel", "arbitrary"),
        name="win_state_update",
    )(flat, flat, new.reshape((n_l * bs, t_dec) + tail))
    return out.reshape(state.shape)


def _oproj_body(o_ref, w_ref, x_ref, g_ref, out_ref):
    y = jnp.dot(o_ref[0].astype(BF16), w_ref[...], preferred_element_type=F32)
    out_ref[0] = x_ref[0] + g_ref[0] * y


def _oproj_merge_body(o0_ref, o1_ref, o2_ref, l0_ref, l1_ref, l2_ref, w_ref, x_ref, g_ref, out_ref):
    l0, l1, l2 = l0_ref[0], l1_ref[0], l2_ref[0]
    lm = jnp.maximum(jnp.maximum(l0, l1), l2)
    e0, e1, e2 = jnp.exp(l0 - lm), jnp.exp(l1 - lm), jnp.exp(l2 - lm)
    o = (e0 * o0_ref[0] + e1 * o1_ref[0] + e2 * o2_ref[0]) / (e0 + e1 + e2)
    y = jnp.dot(o.astype(BF16), w_ref[...], preferred_element_type=F32)
    out_ref[0] = x_ref[0] + g_ref[0] * y


def _oproj(o_list, w, x, gate):
    b, t, d = x.shape
    din = w.shape[0]
    tm = _row_tile(t)
    tmod = gate.shape[1]
    tmb = tm if tmod == t else 1
    gate_map = (lambda bi, mi: (bi, mi, 0)) if tmod == t else (lambda bi, mi: (bi, 0, 0))
    row = lambda bi, mi: (bi, mi, 0)
    body = _oproj_body if len(o_list) == 1 else _oproj_merge_body
    return pl.pallas_call(
        body,
        grid=(b, t // tm),
        in_specs=[pl.BlockSpec((1, tm, din), row)] * len(o_list) + [
            pl.BlockSpec((din, d), lambda bi, mi: (0, 0)),
            pl.BlockSpec((1, tm, d), row),
            pl.BlockSpec((1, tmb, d), gate_map)],
        out_specs=pl.BlockSpec((1, tm, d), row),
        out_shape=jax.ShapeDtypeStruct((b, t, d), F32),
        compiler_params=_cparams("parallel", "parallel"),
        name="out_proj_residual",
    )(*o_list, w, x, gate)


def _conv_ffn_body(x_ref, sh_ref, sc_ref, gt_ref, g_ref, win_ref, cw_ref, cb_ref, wd_ref, p1_ref, p2_ref,
                   out_ref, a_ref, z_s, carry_s, *, chunk, seq_len, carry_rows):
    tm = x_ref.shape[1]
    dff = wd_ref.shape[0]
    x = x_ref[0]
    h = _norm_mod(x, g_ref[...], sh_ref[0], sc_ref[0]).astype(BF16)
    row = lax.broadcasted_iota(jnp.int32, (tm, chunk), 0)
    t_in_seq = row % seq_len
    if carry_rows:
        @pl.when(pl.program_id(1) == 0)
        def _():
            carry_s[...] = jnp.zeros(carry_s.shape, F32)

    for c in range(dff // chunk):
        cs = slice(c * chunk, (c + 1) * chunk)
        a = jnp.dot(h, win_ref[:, cs], preferred_element_type=F32)
        bgate = jnp.dot(h, win_ref[:, dff + c * chunk:dff + (c + 1) * chunk], preferred_element_type=F32)
        a1 = jnp.where(t_in_seq >= 1, pltpu.roll(a, 1, axis=0), 0.0)
        a2 = jnp.where(t_in_seq >= 2, pltpu.roll(a, 2, axis=0), 0.0)
        if carry_rows:
            prev = carry_s[:, cs]
            a1 = jnp.where(row == 0, prev[SUBLANES - 1:SUBLANES, :], a1)
            a2 = jnp.where(row == 0, prev[SUBLANES - 2:SUBLANES - 1, :],
                           jnp.where(row == 1, prev[SUBLANES - 1:SUBLANES, :], a2))
            carry_s[:, cs] = a[tm - SUBLANES:, :]
            a_ref[0, :, cs] = a[tm - SUBLANES:, :]
        else:
            a1 = a1 + p1_ref[0, :, cs]
            a2 = a2 + p2_ref[0, :, cs]
            a_ref[0, :, cs] = a
        y = cb_ref[:, cs] + cw_ref[0:1, cs] * a2 + cw_ref[1:2, cs] * a1 + cw_ref[2:3, cs] * a
        z_s[:, cs] = (y * jax.nn.sigmoid(y) * bgate).astype(BF16)
    f = jnp.dot(z_s[...], wd_ref[...], preferred_element_type=F32)
    out_ref[0] = x + gt_ref[0] * f


def _conv_ffn(x, sh, sc, gate, g, w_in, conv_w, conv_b, w_down, seq_len, p1=None, p2=None):
    b, t, d = x.shape
    dff = w_down.shape[0]
    tm = _row_tile(t)
    carry_rows = p1 is None
    chunk = 256
    assert dff % chunk == 0 and tm % SUBLANES == 0
    tmod = sh.shape[1]
    tmb = tm if tmod == t else 1
    mod_map = (lambda bi, mi: (bi, mi, 0)) if tmod == t else (lambda bi, mi: (bi, 0, 0))
    row = lambda bi, mi: (bi, mi, 0)
    const = lambda bi, mi: (0, 0)
    if carry_rows:
        assert seq_len == t
        p1 = p2 = jnp.zeros((1, SUBLANES, LANES), F32)
        p_spec = pl.BlockSpec((1, SUBLANES, LANES), lambda bi, mi: (0, 0, 0))
        a_rows = SUBLANES
        a_spec = pl.BlockSpec((1, SUBLANES, dff), lambda bi, mi: (bi, 0, 0))
    else:
        assert tm % seq_len == 0
        p_spec = pl.BlockSpec((1, tm, dff), row)
        a_rows = t
        a_spec = pl.BlockSpec((1, tm, dff), row)
    return pl.pallas_call(
        functools.partial(_conv_ffn_body, chunk=chunk, seq_len=seq_len, carry_rows=carry_rows),
        grid=(b, t // tm),
        in_specs=[pl.BlockSpec((1, tm, d), row),
                  pl.BlockSpec((1, tmb, d), mod_map),
                  pl.BlockSpec((1, tmb, d), mod_map),
                  pl.BlockSpec((1, tmb, d), mod_map),
                  pl.BlockSpec((1, d), const),
                  pl.BlockSpec((d, 2 * dff), const),
                  pl.BlockSpec((CONV_W, dff), const),
                  pl.BlockSpec((1, dff), const),
                  pl.BlockSpec((dff, d), const),
                  p_spec, p_spec],
        out_specs=[pl.BlockSpec((1, tm, d), row), a_spec],
        out_shape=[jax.ShapeDtypeStruct((b, t, d), F32),
                   jax.ShapeDtypeStruct((b, a_rows, dff), F32)],
        scratch_shapes=[pltpu.VMEM((tm, dff), BF16),
                        pltpu.VMEM((SUBLANES, dff), F32)],
        compiler_params=_cparams("parallel", "arbitrary"),
        name="conv_ffn",
    )(x, sh, sc, gate, g, w_in, conv_w, conv_b, w_down, p1, p2)


def _final_norm_body(x_ref, g_ref, o_ref):
    x = x_ref[0]
    ms = jnp.mean(x * x, axis=-1, keepdims=True)
    o_ref[0] = x * lax.rsqrt(ms + RMS_EPS) * g_ref[...]


def _final_norm(x, g):
    b, t, d = x.shape
    tm = _row_tile(t)
    return pl.pallas_call(
        _final_norm_body,
        grid=(b, t // tm),
        in_specs=[pl.BlockSpec((1, tm, d), lambda bi, mi: (bi, mi, 0)),
                  pl.BlockSpec((1, d), lambda bi, mi: (0, 0))],
        out_specs=pl.BlockSpec((1, tm, d), lambda bi, mi: (bi, mi, 0)),
        out_shape=jax.ShapeDtypeStruct((b, t, d), F32),
        compiler_params=_cparams("parallel", "parallel"),
        name="final_norm",
    )(x, g)


def _alibi_slopes(n_heads):
    return jnp.exp2(-8.0 * jnp.arange(1, n_heads + 1, dtype=F32) / n_heads)


def kernel(x_prompt, x_sample, cache_moba_k, cache_moba_v, page_table, state_win1, state_win2, state_win3,
           state_conv, c_prompt, c_sample, ada_w, ada_b, norm1_g, norm2_g, final_g, a_w_qkv, a_w_o,
           b_w_qkv, b_w_o, ffn_w_in, ffn_conv_w, ffn_conv_b, ffn_w_down):
    depth = ada_w.shape[0]
    bp, tp, d = x_prompt.shape
    bs, ts, _ = x_sample.shape
    n_heads_a = d // HEAD_DIM_A
    dff = ffn_w_down.shape[1]
    states = (state_win1, state_win2, state_win3)

    slopes_a = _alibi_slopes(n_heads_a)
    slopes_b = _alibi_slopes(N_GROUPS_B * HEADS_PER_GROUP_B)
    row = jnp.arange(ts * n_heads_a)
    slope_rows = jnp.broadcast_to(slopes_a[row % n_heads_a][:, None], (ts * n_heads_a, MOBA_BLOCK)).astype(F32)
    tq_rows = jnp.broadcast_to((row // n_heads_a)[:, None], (ts * n_heads_a, MOBA_BLOCK)).astype(F32)

    mods = _ada_all(jnp.concatenate([c_prompt, c_sample], axis=0), ada_w, ada_b)
    mods = mods.reshape(depth, bp + bs, 6, d)

    cache_kt = jnp.transpose(cache_moba_k, (0, 1, 3, 4, 2))
    cache_vt = jnp.transpose(cache_moba_v, (0, 1, 3, 4, 2))

    xp = x_prompt
    xs = x_sample.reshape(1, bs * ts, d)
    ka_p, va_p, ka_s, va_s = [], [], [], []
    win_p = [[] for _ in range(N_GROUPS_B)]
    win_new = [[] for _ in range(N_GROUPS_B)]
    conv_p, conv_s = [], []
    for i in range(depth):
        mp = [mods[i, :bp, k].reshape(bp, 1, d) for k in range(6)]
        ms = [jnp.repeat(mods[i, bp:, k], ts, axis=0).reshape(1, bs * ts, d) for k in range(6)]
        g1 = norm1_g[i].reshape(1, d)
        g2 = norm2_g[i].reshape(1, d)
        j = i // 2
        if i % 2 == 0:
            w_o = a_w_o[j].astype(BF16)
            q_p, kb_p, kt_p, vt_p, vtb_p, km_p = _moba_qkv_prompt(xp, mp[0], mp[1], g1, a_w_qkv[j])
            qkv_s = _nm_matmul(xs, ms[0], ms[1], g1, a_w_qkv[j].astype(BF16)).reshape(bs, ts, 3 * d)
            o_p = _moba_prompt_attn(q_p, kb_p, vtb_p, km_p, slopes_a)
            o_s = _moba_sample_attn(qkv_s, cache_kt, cache_vt, j, page_table, slope_rows, tq_rows)
            xp = _oproj([o_p], w_o, xp, mp[2])
            xs = _oproj([o_s.reshape(1, bs * ts, d)], w_o, xs, ms[2])
            ka_p.append(kt_p.reshape(bp, n_heads_a, HEAD_DIM_A, tp))
            va_p.append(vt_p.reshape(bp, n_heads_a, HEAD_DIM_A, tp))
            ka_s.append(qkv_s[:, :, d:2 * d].reshape(bs, ts, n_heads_a, HEAD_DIM_A))
            va_s.append(qkv_s[:, :, 2 * d:].reshape(bs, ts, n_heads_a, HEAD_DIM_A))
        else:
            w_o = b_w_o[j].astype(BF16)
            views_p = _dilated_qkv_prompt(xp, mp[0], mp[1], g1, b_w_qkv[j])
            qkv_s = _nm_matmul(xs, ms[0], ms[1], g1, b_w_qkv[j].astype(BF16)).reshape(bs, ts, -1)
            parts = [_dilated_prompt_group(views_p[g], slopes_b, g) for g in range(N_GROUPS_B)]
            xp = _oproj([p[0] for p in parts] + [p[1] for p in parts], w_o, xp, mp[2])
            o_s = _dilated_sample_attn(qkv_s, states, j, slopes_b)
            xs = _oproj([o_s.reshape(1, bs * ts, D_GROUP_B)], w_o, xs, ms[2])
            kv_s = qkv_s.reshape(bs, ts, 3, N_GROUPS_B, HEADS_PER_GROUP_B, HEAD_DIM_B)[:, :, 1:]
            for g in range(N_GROUPS_B):
                wp = min(WINDOWS[g], tp)
                dil = DILATIONS[g]
                assert wp % dil == 0
                tail = views_p[g][:, (tp - wp) // dil:].reshape(
                    bp, wp, 3, HEADS_PER_GROUP_B, HEAD_DIM_B)
                win_p[g].append(tail[:, :, 1:])
                win_new[g].append(kv_s[:, :, :, g])
        w_in = ffn_w_in[i].astype(BF16)
        w_down = ffn_w_down[i].astype(BF16)
        cw = ffn_conv_w[i]
        cb = ffn_conv_b[i].reshape(1, dff)
        xp, a_tail = _conv_ffn(xp, mp[3], mp[4], mp[5], g2, w_in, cw, cb, w_down, tp)
        conv_p.append(a_tail[:, -(CONV_W - 1):])
        st = state_conv[i]
        zero = jnp.zeros((bs, 1, dff), F32)
        p1 = jnp.concatenate([st[:, 1:2], jnp.tile(zero, (1, ts - 1, 1))], axis=1).reshape(1, bs * ts, dff)
        p2 = jnp.concatenate([st[:, 0:1], st[:, 1:2], jnp.tile(zero, (1, ts - 2, 1))], axis=1).reshape(1, bs * ts, dff)
        xs, a_s = _conv_ffn(xs, ms[3], ms[4], ms[5], g2, w_in, cw, cb, w_down, ts, p1, p2)
        a_ext = jnp.concatenate([st, a_s.reshape(bs, ts, dff)], axis=1)
        conv_s.append(a_ext[:, -(CONV_W - 1):])
    y_prompt = _final_norm(xp, final_g.reshape(1, d))
    y_sample = _final_norm(xs, final_g.reshape(1, d)).reshape(bs, ts, d)
    win_s = [_win_update(states[g], jnp.stack(win_new[g])) for g in range(N_GROUPS_B)]
    moba_k_prompt = jnp.transpose(jnp.stack(ka_p), (0, 1, 4, 2, 3))
    moba_v_prompt = jnp.transpose(jnp.stack(va_p), (0, 1, 4, 2, 3))
    return (y_prompt, y_sample, moba_k_prompt, moba_v_prompt, jnp.stack(ka_s), jnp.stack(va_s),
            jnp.stack(win_p[0]), jnp.stack(win_p[1]), jnp.stack(win_p[2]),
            win_s[0], win_s[1], win_s[2],
            jnp.stack(conv_p), jnp.stack(conv_s))
```

```python
import functools

import jax
import jax.numpy as jnp
from jax import lax
from jax.experimental import pallas as pl
from jax.experimental.pallas import tpu as pltpu

F32 = jnp.float32
BF16 = jnp.bfloat16

HEAD_DIM_A = 64
MOBA_BLOCK = 256
MOBA_TOPK = 3
HEAD_DIM_B = 128
N_GROUPS_B = 3
HEADS_PER_GROUP_B = 4
D_GROUP_B = HEADS_PER_GROUP_B * HEAD_DIM_B
WINDOWS = (128, 512, 2048)
DILATIONS = (1, 4, 16)
CONV_W = 3
RMS_EPS = 1e-6
NEG_INF = -1e30
POS_BIG = 1e30
LOG2E = 1.4426950408889634

LANES = 128
SUBLANES = 8
VMEM_LIMIT = 56 * 1024 * 1024

MOBA_HEADS_PER_STEP = 8
MOBA_BLOCKS_PER_ITER = 2
MOBA_PAGES_PER_STEP = 16

_NT = (((1,), (1,)), ((), ()))


def _cparams(*sem):
    return pltpu.CompilerParams(dimension_semantics=sem, vmem_limit_bytes=VMEM_LIMIT)


def _row_tile(t, cap=512):
    tm = min(t, cap)
    assert t % tm == 0
    return tm


def _norm_mod(x, g, sh, sc):
    ms = jnp.mean(x * x, axis=-1, keepdims=True)
    y = x * lax.rsqrt(ms + RMS_EPS) * g
    return y * (1.0 + sc) + sh


def _ada_body(c_ref, w_ref, b_ref, o_ref):
    c = c_ref[...]
    s = (c * jax.nn.sigmoid(c)).astype(BF16)
    o_ref[0] = jnp.dot(s, w_ref[0].astype(BF16), preferred_element_type=F32) + b_ref[0]


def _ada_all(c_all, ada_w, ada_b):
    depth, d, n6 = ada_w.shape
    nb = c_all.shape[0]
    tn = n6 // 4
    return pl.pallas_call(
        _ada_body,
        grid=(depth, n6 // tn),
        in_specs=[pl.BlockSpec((nb, d), lambda l, n: (0, 0)),
                  pl.BlockSpec((1, d, tn), lambda l, n: (l, 0, n)),
                  pl.BlockSpec((1, 1, tn), lambda l, n: (l, 0, n))],
        out_specs=pl.BlockSpec((1, nb, tn), lambda l, n: (l, 0, n)),
        out_shape=jax.ShapeDtypeStruct((depth, nb, n6), F32),
        compiler_params=_cparams("parallel", "parallel"),
        name="ada_table",
    )(c_all, ada_w, ada_b.reshape(depth, 1, n6))


def _nm_matmul_body(x_ref, sh_ref, sc_ref, g_ref, w_ref, o_ref, h_s):
    @pl.when(pl.program_id(2) == 0)
    def _():
        h_s[...] = _norm_mod(x_ref[0], g_ref[...], sh_ref[0], sc_ref[0]).astype(BF16)

    o_ref[0] = jnp.dot(h_s[...], w_ref[...], preferred_element_type=F32)


def _nm_matmul(x, sh, sc, g, w, n_tiles=3):
    b, t, d = x.shape
    n = w.shape[1]
    tm = _row_tile(t)
    tmod = sh.shape[1]
    tmb = tm if tmod == t else 1
    mod_map = (lambda bi, mi, ni: (bi, mi, 0)) if tmod == t else (lambda bi, mi, ni: (bi, 0, 0))
    tn = n // n_tiles
    return pl.pallas_call(
        _nm_matmul_body,
        grid=(b, t // tm, n_tiles),
        in_specs=[pl.BlockSpec((1, tm, d), lambda bi, mi, ni: (bi, mi, 0)),
                  pl.BlockSpec((1, tmb, d), mod_map),
                  pl.BlockSpec((1, tmb, d), mod_map),
                  pl.BlockSpec((1, d), lambda bi, mi, ni: (0, 0)),
                  pl.BlockSpec((d, tn), lambda bi, mi, ni: (0, ni))],
        out_specs=pl.BlockSpec((1, tm, tn), lambda bi, mi, ni: (bi, mi, ni)),
        out_shape=jax.ShapeDtypeStruct((b, t, n), F32),
        scratch_shapes=[pltpu.VMEM((tm, d), BF16)],
        compiler_params=_cparams("parallel", "parallel", "arbitrary"),
        name="norm_mod_matmul",
    )(x, sh, sc, g, w)


def _moba_qkv_body(x_ref, sh_ref, sc_ref, g_ref, wq_ref, wk_ref, wkt_ref, wvt_ref,
                   q_ref, kb_ref, kt_ref, vt_ref, vtb_ref, km_ref):
    tm = x_ref.shape[1]
    per_tile = tm // MOBA_BLOCK
    h = _norm_mod(x_ref[0], g_ref[...], sh_ref[0], sc_ref[0]).astype(BF16)
    q_ref[0] = jnp.dot(h, wq_ref[...], preferred_element_type=F32)
    k = jnp.dot(h, wk_ref[...], preferred_element_type=F32)
    kb_ref[0] = k.astype(BF16)
    first = pl.program_id(1) * per_tile
    for j in range(per_tile):
        km_ref[0, pl.ds(first + j, 1), :] = (
            jnp.sum(k[j * MOBA_BLOCK:(j + 1) * MOBA_BLOCK], axis=0, keepdims=True) * (1.0 / MOBA_BLOCK))
    kt_ref[0] = lax.dot_general(wkt_ref[...], h, _NT, preferred_element_type=F32)
    vt = lax.dot_general(wvt_ref[...], h, _NT, preferred_element_type=F32)
    vt_ref[0] = vt
    for j in range(per_tile):
        vtb_ref[0, j] = vt[:, j * MOBA_BLOCK:(j + 1) * MOBA_BLOCK].astype(BF16)


def _moba_qkv_prompt(x, sh, sc, g, w_qkv):
    b, t, d = x.shape
    tm = _row_tile(t)
    assert tm % MOBA_BLOCK == 0
    nb = t // MOBA_BLOCK
    wq = w_qkv[:, :d].astype(BF16)
    wk = w_qkv[:, d:2 * d].astype(BF16)
    wkt = w_qkv[:, d:2 * d].T.astype(BF16)
    wvt = w_qkv[:, 2 * d:].T.astype(BF16)
    row = lambda bi, mi: (bi, mi, 0)
    col = lambda bi, mi: (bi, 0, mi)
    mod = lambda bi, mi: (bi, 0, 0)
    const = lambda bi, mi: (0, 0)
    return pl.pallas_call(
        _moba_qkv_body,
        grid=(b, t // tm),
        in_specs=[pl.BlockSpec((1, tm, d), row),
                  pl.BlockSpec((1, 1, d), mod),
                  pl.BlockSpec((1, 1, d), mod),
                  pl.BlockSpec((1, d), const)] + [pl.BlockSpec((d, d), const)] * 4,
        out_specs=[pl.BlockSpec((1, tm, d), row),
                   pl.BlockSpec((1, tm, d), row),
                   pl.BlockSpec((1, d, tm), col),
                   pl.BlockSpec((1, d, tm), col),
                   pl.BlockSpec((1, tm // MOBA_BLOCK, d, MOBA_BLOCK), lambda bi, mi: (bi, mi, 0, 0)),
                   pl.BlockSpec((1, nb, d), mod)],
        out_shape=[jax.ShapeDtypeStruct((b, t, d), F32),
                   jax.ShapeDtypeStruct((b, t, d), BF16),
                   jax.ShapeDtypeStruct((b, d, t), F32),
                   jax.ShapeDtypeStruct((b, d, t), F32),
                   jax.ShapeDtypeStruct((b, nb, d, MOBA_BLOCK), BF16),
                   jax.ShapeDtypeStruct((b, nb, d), F32)],
        compiler_params=_cparams("parallel", "arbitrary"),
        name="moba_qkv_prompt",
    )(x, sh, sc, g, wq, wk, wkt, wvt)


def _topk_rank_select(gate, blk, n_valid, n_rows):
    gate = jnp.where(blk < n_valid, gate, NEG_INF)
    rank = jnp.zeros(gate.shape, F32)
    for m in range(n_rows):
        gm = gate[m:m + 1, :]
        beats = (gm > gate) | ((gm == gate) & (blk > m))
        rank = rank + jnp.where(beats, 1.0, 0.0)
    n_sel = jnp.minimum(n_valid, MOBA_TOPK).astype(F32)
    return jnp.where((rank < n_sel) & (blk < n_valid), 1.0, 0.0)


def _moba_prompt_body(slopes_ref, q_ref, kb_ref, vtb_ref, km_ref, o_ref,
                      d0_s, sel_s, qb_s, m_s, l_s, acc_s, s_s, *, nb, hps, per_iter):
    blk_sz, hd = MOBA_BLOCK, HEAD_DIM_A
    head0 = pl.program_id(1) * hps
    i = pl.program_id(2)

    @pl.when(i == 0)
    def _init():
        rk = lax.broadcasted_iota(jnp.int32, (blk_sz, blk_sz), 0)
        rq = lax.broadcasted_iota(jnp.int32, (blk_sz, blk_sz), 1)
        rel = (rk - rq).astype(F32)
        for h in range(hps):
            d0 = (slopes_ref[head0 + h] * LOG2E) * rel
            d0_s[h, 0] = d0
            d0_s[h, 1] = jnp.where(rk <= rq, d0, NEG_INF)

    lane = lax.broadcasted_iota(jnp.int32, (1, LANES), 1)
    blk = lax.broadcasted_iota(jnp.int32, (nb, blk_sz), 0)
    for h in range(hps):
        pair, hh = divmod(h, LANES // hd)
        cs = slice(pair * LANES, (pair + 1) * LANES)
        qh = jnp.where((lane >= hh * hd) & (lane < (hh + 1) * hd), q_ref[0, :, cs] * (hd ** -0.5 * LOG2E), 0.0)
        q_hi = qh.astype(BF16)
        q_lo = (qh - q_hi.astype(F32)).astype(BF16)
        km = km_ref[0, :, cs]
        km_hi = km.astype(BF16)
        km_lo = (km - km_hi.astype(F32)).astype(BF16)
        g2 = lax.dot_general(jnp.concatenate([km_hi, km_lo], axis=0), q_hi, _NT, preferred_element_type=F32)
        gate = g2[:nb] + g2[nb:] + lax.dot_general(km_hi, q_lo, _NT, preferred_element_type=F32)
        sel_s[h] = jnp.where(blk == i, 1.0, _topk_rank_select(gate, blk, i, nb))
        qb_s[h] = q_hi
        m_s[h] = jnp.full((1, blk_sz), NEG_INF, F32)
        l_s[h] = jnp.zeros((1, blk_sz), F32)
        acc_s[h] = jnp.zeros((hd, blk_sz), F32)

    def key_blocks(it, carry):
        blocks = []
        for u in range(per_iter):
            n = it * per_iter + u
            n_mem = jnp.minimum(n, nb - 1)
            rows = pl.ds(pl.multiple_of(n_mem * blk_sz, blk_sz), blk_sz)
            own = (n == i).astype(jnp.int32)
            for h in range(hps):
                pair = h // (LANES // hd)
                kblk = kb_ref[0, rows, pair * LANES:(pair + 1) * LANES]
                s_s[u, h] = (lax.dot_general(kblk, qb_s[h], _NT, preferred_element_type=F32)
                             + d0_s[h, own])
            blocks.append((n, n_mem))
        for u, (n, n_mem) in enumerate(blocks):
            off = ((i - n) * blk_sz).astype(F32)
            for h in range(hps):
                c_n = -(slopes_ref[head0 + h] * LOG2E) * off
                selrow = (sel_s[h, pl.ds(n_mem, 1), :] > 0.5) & (n <= i)
                m_old = m_s[h]
                cmax = jnp.max(s_s[u, h], axis=0, keepdims=True)
                m_new = jnp.maximum(m_old, jnp.where(selrow, cmax + c_n, NEG_INF))
                alpha = jnp.exp2(m_old - m_new)
                p = jnp.exp2(s_s[u, h] - jnp.where(selrow, m_new - c_n, POS_BIG))
                l_s[h] = alpha * l_s[h] + jnp.sum(p, axis=0, keepdims=True)
                vt = vtb_ref[0, n_mem, h * hd:(h + 1) * hd, :]
                acc_s[h] = alpha * acc_s[h] + jnp.dot(vt, p.astype(BF16), preferred_element_type=F32)
                m_s[h] = m_new
        return carry

    lax.fori_loop(0, (i + per_iter) // per_iter, key_blocks, 0)
    per_pair = LANES // hd
    for pair in range(hps // per_pair):
        outs = [acc_s[pair * per_pair + hh] / l_s[pair * per_pair + hh] for hh in range(per_pair)]
        o_ref[0, :, pair * LANES:(pair + 1) * LANES] = jnp.concatenate(outs, axis=0).T.astype(o_ref.dtype)


def _moba_prompt_attn(q, kb, vtb, km, slopes):
    b, t, d = q.shape
    nb = t // MOBA_BLOCK
    hps = MOBA_HEADS_PER_STEP
    w = hps * HEAD_DIM_A
    assert t % MOBA_BLOCK == 0 and d % w == 0 and w % LANES == 0
    return pl.pallas_call(
        functools.partial(_moba_prompt_body, nb=nb, hps=hps, per_iter=MOBA_BLOCKS_PER_ITER),
        grid=(b, d // w, nb),
        in_specs=[pl.BlockSpec(memory_space=pltpu.SMEM),
                  pl.BlockSpec((1, MOBA_BLOCK, w), lambda bi, hg, i: (bi, i, hg)),
                  pl.BlockSpec((1, t, w), lambda bi, hg, i: (bi, 0, hg)),
                  pl.BlockSpec((1, nb, w, MOBA_BLOCK), lambda bi, hg, i: (bi, 0, hg, 0)),
                  pl.BlockSpec((1, nb, w), lambda bi, hg, i: (bi, 0, hg))],
        out_specs=pl.BlockSpec((1, MOBA_BLOCK, w), lambda bi, hg, i: (bi, i, hg)),
        out_shape=jax.ShapeDtypeStruct((b, t, d), BF16),
        scratch_shapes=[pltpu.VMEM((hps, 2, MOBA_BLOCK, MOBA_BLOCK), F32),
                        pltpu.VMEM((hps, nb, MOBA_BLOCK), F32),
                        pltpu.VMEM((hps, MOBA_BLOCK, LANES), BF16),
                        pltpu.VMEM((hps, 1, MOBA_BLOCK), F32),
                        pltpu.VMEM((hps, 1, MOBA_BLOCK), F32),
                        pltpu.VMEM((hps, HEAD_DIM_A, MOBA_BLOCK), F32),
                        pltpu.VMEM((MOBA_BLOCKS_PER_ITER, hps, MOBA_BLOCK, MOBA_BLOCK), F32)],
        compiler_params=_cparams("parallel", "parallel", "arbitrary"),
        name="moba_prompt_attn",
    )(slopes, q, kb, vtb, km)


def _head_block_diag(q, n_heads, hd):
    d = n_heads * hd
    n_t = LANES // n_heads
    row_h = lax.broadcasted_iota(jnp.int32, (n_heads, d), 0)
    col_h = lax.broadcasted_iota(jnp.int32, (n_heads, d), 1) // hd
    pieces = []
    for t in range(n_t):
        if t < q.shape[0]:
            pieces.append(jnp.where(row_h == col_h, jnp.broadcast_to(q[t:t + 1, :], (n_heads, d)), 0.0))
        else:
            pieces.append(jnp.zeros((n_heads, d), F32))
    return jnp.concatenate(pieces, axis=0)


def _moba_s_scores_body(pt_ref, q_ref, *refs, n_heads, t_dec, pages):
    k_refs, s_ref, qs_s = refs[:pages], refs[pages], refs[pages + 1]
    hd = HEAD_DIM_A
    d = n_heads * hd
    rows = t_dec * n_heads
    page = k_refs[0].shape[-1]
    per_blk = MOBA_BLOCK // page

    @pl.when(pl.program_id(1) == 0)
    def _():
        qbd = _head_block_diag(q_ref[0] * (hd ** -0.5), n_heads, hd)[:rows]
        hi = qbd.astype(BF16)
        qs_s[0:rows] = hi
        qs_s[rows:2 * rows] = (qbd - hi.astype(F32)).astype(BF16)

    for p, k_ref in enumerate(k_refs):
        lanes = slice((p % per_blk) * page, (p % per_blk + 1) * page)
        kp = k_ref[0, 0].reshape(d, page)
        k_hi = kp.astype(BF16)
        k_lo = (kp - k_hi.astype(F32)).astype(BF16)
        a = jnp.dot(qs_s[...], k_hi, preferred_element_type=F32)
        b = jnp.dot(qs_s[0:rows], k_lo, preferred_element_type=F32)
        s_ref[0, p // per_blk, :, lanes] = a[:rows] + a[rows:] + b


def _moba_s_softmax_body(s_ref, q_ref, kn_ref, sl_ref, tq_ref, p_ref, po_ref, l_ref,
                         *, n_heads, n_blocks, t_dec, past_len):
    hd = HEAD_DIM_A
    rows = t_dec * n_heads
    blk = MOBA_BLOCK
    lane = lax.broadcasted_iota(jnp.int32, (rows, LANES), 1)
    gate = jnp.full((rows, LANES), NEG_INF, F32)
    for n in range(n_blocks):
        gate = jnp.where(lane == n, jnp.sum(s_ref[0, n], axis=1, keepdims=True), gate)
    rank = jnp.zeros((rows, LANES), F32)
    for m in range(n_blocks):
        gm = gate[:, m:m + 1]
        beats = (gm > gate) | ((gm == gate) & (lane > m))
        rank = rank + jnp.where(beats, 1.0, 0.0)
    sel = (rank < min(MOBA_TOPK, n_blocks)) & (lane < n_blocks)

    sl = sl_ref[...]
    tq = tq_ref[...]
    tok = lax.broadcasted_iota(jnp.int32, (rows, blk), 1).astype(F32)
    dist0 = past_len + tq - tok
    sl1, tq1 = sl[:, 0:1], tq[:, 0:1]

    qbd = _head_block_diag(q_ref[0] * (hd ** -0.5), n_heads, hd)[:rows]
    kn = kn_ref[0]
    own = []
    for tk in range(t_dec):
        s = jnp.sum(qbd * kn[tk:tk + 1, :], axis=1, keepdims=True)
        dd = tq1 - tk
        own.append(jnp.where(dd >= 0, s - sl1 * dd, NEG_INF))
    m = own[0]
    for s in own[1:]:
        m = jnp.maximum(m, s)
    for n in range(n_blocks):
        sb = s_ref[0, n] - sl * (dist0 - n * blk)
        m = jnp.maximum(m, jnp.where(sel[:, n:n + 1], jnp.max(sb, axis=1, keepdims=True), NEG_INF))
    l = jnp.zeros((rows, 1), F32)
    for n in range(n_blocks):
        sb = s_ref[0, n] - sl * (dist0 - n * blk)
        p = jnp.exp(sb - jnp.where(sel[:, n:n + 1], m, POS_BIG))
        l = l + jnp.sum(p, axis=1, keepdims=True)
        p_ref[0, n, 0:rows, :] = p.astype(BF16)
        p_ref[0, n, rows:, :] = jnp.zeros((p_ref.shape[2] - rows, blk), BF16)
    po = jnp.zeros((rows, LANES), F32)
    for tk in range(t_dec):
        pk = jnp.exp(own[tk] - m)
        l = l + pk
        po = jnp.where(lane == tk, pk, po)
    po_ref[0] = po
    l_ref[0] = jnp.broadcast_to(l, (rows, LANES))


def _moba_s_pv_body(pt_ref, p_ref, po_ref, l_ref, vn_ref, *refs, n_heads, t_dec, pages):
    v_refs, o_ref, acc_s = refs[:pages], refs[pages], refs[pages + 1]
    hd = HEAD_DIM_A
    n = pl.program_id(1)
    page = v_refs[0].shape[-1]
    per_blk = MOBA_BLOCK // page
    d = n_heads * hd
    rows = t_dec * n_heads

    @pl.when(n == 0)
    def _():
        acc_s[...] = jnp.zeros(acc_s.shape, F32)

    tot = None
    for p, v_ref in enumerate(v_refs):
        vt = v_ref[0, 0].reshape(d, page).astype(BF16)
        pp = p_ref[0, p // per_blk][:, (p % per_blk) * page:(p % per_blk + 1) * page]
        y = lax.dot_general(vt, pp, _NT, preferred_element_type=F32)
        tot = y if tot is None else tot + y
    acc_s[...] += tot

    @pl.when(n == pl.num_programs(1) - 1)
    def _():
        acc = acc_s[...].T[:rows]
        po = po_ref[0]
        vn = vn_ref[0]
        for tk in range(t_dec):
            acc = acc + po[:, tk:tk + 1] * vn[tk:tk + 1, :]
        row_h = lax.broadcasted_iota(jnp.int32, (rows, d), 0) % n_heads
        col_h = lax.broadcasted_iota(jnp.int32, (rows, d), 1) // hd
        acc = jnp.where(row_h == col_h, acc / l_ref[0][:, 0:1], 0.0)
        for t in range(t_dec):
            o_ref[0, t:t + 1, :] = jnp.sum(acc[t * n_heads:(t + 1) * n_heads, :], axis=0, keepdims=True)


def _moba_sample_attn(qkv_s, cache_kt, cache_vt, layer, page_table, slope_rows, tq_rows):
    bs, t_dec, d3 = qkv_s.shape
    d = d3 // 3
    n_heads, hd, page = cache_kt.shape[2:]
    n_pages = page_table.shape[1]
    past_len = n_pages * page
    pages = MOBA_PAGES_PER_STEP
    rows = t_dec * n_heads
    assert hd == HEAD_DIM_A and n_heads * hd == d
    assert MOBA_BLOCK % page == 0 and past_len % MOBA_BLOCK == 0 and page == LANES
    assert rows % SUBLANES == 0 and rows <= LANES and t_dec <= SUBLANES
    n_blocks = past_len // MOBA_BLOCK
    blocks_per_step = pages * page // MOBA_BLOCK
    assert MOBA_TOPK <= n_blocks <= LANES and n_pages % pages == 0 and (pages * page) % MOBA_BLOCK == 0
    n_steps = n_pages // pages
    pad = SUBLANES - t_dec
    q8 = jnp.pad(qkv_s[:, :, :d], ((0, 0), (0, pad), (0, 0)))
    k8 = jnp.pad(qkv_s[:, :, d:2 * d], ((0, 0), (0, pad), (0, 0)))
    v8 = jnp.pad(qkv_s[:, :, 2 * d:], ((0, 0), (0, pad), (0, 0)))

    def page_spec(p):
        return pl.BlockSpec((1, 1, n_heads, hd, page),
                            lambda bi, n, pt: (layer, pt[bi, pages * n + p], 0, 0, 0))

    small = lambda bi, n, pt: (bi, 0, 0)
    s_all = pl.pallas_call(
        functools.partial(_moba_s_scores_body, n_heads=n_heads, t_dec=t_dec, pages=pages),
        grid_spec=pltpu.PrefetchScalarGridSpec(
            num_scalar_prefetch=1,
            grid=(bs, n_steps),
            in_specs=[pl.BlockSpec((1, SUBLANES, d), small)] + [page_spec(p) for p in range(pages)],
            out_specs=pl.BlockSpec((1, blocks_per_step, rows, MOBA_BLOCK), lambda bi, n, pt: (bi, n, 0, 0)),
            scratch_shapes=[pltpu.VMEM((2 * rows, d), BF16)]),
        out_shape=jax.ShapeDtypeStruct((bs, n_blocks, rows, MOBA_BLOCK), F32),
        compiler_params=_cparams("parallel", "arbitrary"),
        name="moba_sample_scores",
    )(page_table, q8, *([cache_kt] * pages))

    per_b = lambda bi: (bi, 0, 0)
    const = lambda bi: (0, 0)
    p_all, p_own, l_sum = pl.pallas_call(
        functools.partial(_moba_s_softmax_body, n_heads=n_heads, n_blocks=n_blocks, t_dec=t_dec,
                          past_len=past_len),
        grid=(bs,),
        in_specs=[pl.BlockSpec((1, n_blocks, rows, MOBA_BLOCK), lambda bi: (bi, 0, 0, 0)),
                  pl.BlockSpec((1, SUBLANES, d), per_b),
                  pl.BlockSpec((1, SUBLANES, d), per_b),
                  pl.BlockSpec((rows, MOBA_BLOCK), const),
                  pl.BlockSpec((rows, MOBA_BLOCK), const)],
        out_specs=[pl.BlockSpec((1, n_blocks, LANES, MOBA_BLOCK), lambda bi: (bi, 0, 0, 0)),
                   pl.BlockSpec((1, rows, LANES), per_b),
                   pl.BlockSpec((1, rows, LANES), per_b)],
        out_shape=[jax.ShapeDtypeStruct((bs, n_blocks, LANES, MOBA_BLOCK), BF16),
                   jax.ShapeDtypeStruct((bs, rows, LANES), F32),
                   jax.ShapeDtypeStruct((bs, rows, LANES), F32)],
        compiler_params=_cparams("parallel"),
        name="moba_sample_softmax",
    )(s_all, q8, k8, slope_rows, tq_rows)

    return pl.pallas_call(
        functools.partial(_moba_s_pv_body, n_heads=n_heads, t_dec=t_dec, pages=pages),
        grid_spec=pltpu.PrefetchScalarGridSpec(
            num_scalar_prefetch=1,
            grid=(bs, n_steps),
            in_specs=[pl.BlockSpec((1, blocks_per_step, LANES, MOBA_BLOCK), lambda bi, n, pt: (bi, n, 0, 0)),
                      pl.BlockSpec((1, rows, LANES), small),
                      pl.BlockSpec((1, rows, LANES), small),
                      pl.BlockSpec((1, SUBLANES, d), small)] + [page_spec(p) for p in range(pages)],
            out_specs=pl.BlockSpec((1, t_dec, d), small),
            scratch_shapes=[pltpu.VMEM((d, LANES), F32)]),
        out_shape=jax.ShapeDtypeStruct((bs, t_dec, d), F32),
        compiler_params=_cparams("parallel", "arbitrary"),
        name="moba_sample_pv",
    )(page_table, p_all, p_own, l_sum, v8, *([cache_vt] * pages))


def _dilated_prompt_body(slopes_ref, q_ref, kp_ref, kc_ref, vp_ref, vc_ref, o_ref, lse_ref, bias_s,
                         *, group, dilation, n_keys):
    tq = q_ref.shape[1]
    hd = HEAD_DIM_B
    first = (pl.program_id(0) == 0) & (pl.program_id(1) == 0) & (pl.program_id(2) == 0)

    @pl.when(first)
    def _():
        iq = lax.broadcasted_iota(jnp.int32, (tq, 2 * tq), 0)
        jk = lax.broadcasted_iota(jnp.int32, (tq, 2 * tq), 1)
        steps = iq + tq - jk
        ok = (steps >= 0) & (steps < n_keys)
        dist = (steps * dilation).astype(F32)
        for h in range(HEADS_PER_GROUP_B):
            b = -slopes_ref[group * HEADS_PER_GROUP_B + h] * dist
            bias_s[0, h] = jnp.where(ok & (jk >= tq), b, NEG_INF)
            bias_s[1, h] = jnp.where(ok, b, NEG_INF)

    var = jnp.minimum(pl.program_id(2), 1)
    scale = hd ** -0.5
    def scores(h):
        cs = slice(h * hd, (h + 1) * hd)
        qh = q_ref[0, :, cs].astype(BF16)
        return (lax.dot_general(qh, kp_ref[0, :, cs].astype(BF16), _NT, preferred_element_type=F32),
                lax.dot_general(qh, kc_ref[0, :, cs].astype(BF16), _NT, preferred_element_type=F32))

    s_next = scores(0)
    for h in range(HEADS_PER_GROUP_B):
        cs = slice(h * hd, (h + 1) * hd)
        bias = bias_s[var, h]
        s_p = s_next[0] * scale + bias[:, :tq]
        s_c = s_next[1] * scale + bias[:, tq:]
        if h + 1 < HEADS_PER_GROUP_B:
            s_next = scores(h + 1)
        m = jnp.maximum(jnp.max(s_p, axis=1, keepdims=True), jnp.max(s_c, axis=1, keepdims=True))
        p_p = jnp.exp(s_p - m)
        p_c = jnp.exp(s_c - m)
        l = jnp.sum(p_p, axis=1, keepdims=True) + jnp.sum(p_c, axis=1, keepdims=True)
        o = (jnp.dot(p_p.astype(BF16), vp_ref[0, :, cs].astype(BF16), preferred_element_type=F32)
             + jnp.dot(p_c.astype(BF16), vc_ref[0, :, cs].astype(BF16), preferred_element_type=F32))
        o_ref[0, :, cs] = o / l
        lse_ref[0, :, cs] = jnp.broadcast_to(m + jnp.log(l), (tq, hd))


def _dilated_qkv_body(x_ref, sh_ref, sc_ref, g_ref, w_ref, o0_ref, o1_ref, o2_ref, h_s, r_s):
    grp = pl.program_id(2)
    tm = x_ref.shape[1]

    @pl.when(grp == 0)
    def _():
        h_s[...] = _norm_mod(x_ref[0], g_ref[...], sh_ref[0], sc_ref[0]).astype(BF16)

    res = jnp.dot(h_s[...], w_ref[...], preferred_element_type=F32)
    width = res.shape[1]
    for g, o_ref in enumerate((o0_ref, o1_ref, o2_ref)):
        dil = DILATIONS[g]

        @pl.when(grp == g)
        def _(o_ref=o_ref, dil=dil):
            if dil == 1:
                o_ref[0] = res
            else:
                for c in range(width // LANES):
                    r_s[c] = res[:, c * LANES:(c + 1) * LANES]
                for r in range(dil):
                    for c in range(width // LANES):
                        o_ref[0, :, r * width + c * LANES:r * width + (c + 1) * LANES] = (
                            r_s[c, pl.ds(r, tm // dil, stride=dil), :])


def _dilated_qkv_prompt(x, sh, sc, g, w_qkv):
    b, t, d = x.shape
    dg = D_GROUP_B
    tm = _row_tile(t)
    assert all(tm % dil == 0 and (tm // dil) % SUBLANES == 0 for dil in DILATIONS)
    w = w_qkv.reshape(d, 3, N_GROUPS_B, dg).transpose(0, 2, 1, 3).reshape(d, N_GROUPS_B * 3 * dg).astype(BF16)
    mod = lambda bi, mi, gi: (bi, 0, 0)
    return pl.pallas_call(
        _dilated_qkv_body,
        grid=(b, t // tm, N_GROUPS_B),
        in_specs=[pl.BlockSpec((1, tm, d), lambda bi, mi, gi: (bi, mi, 0)),
                  pl.BlockSpec((1, 1, d), mod),
                  pl.BlockSpec((1, 1, d), mod),
                  pl.BlockSpec((1, d), lambda bi, mi, gi: (0, 0)),
                  pl.BlockSpec((d, 3 * dg), lambda bi, mi, gi: (0, gi))],
        out_specs=[pl.BlockSpec((1, tm // dil, dil * 3 * dg), lambda bi, mi, gi: (bi, mi, 0))
                   for dil in DILATIONS],
        out_shape=[jax.ShapeDtypeStruct((b, t // dil, dil * 3 * dg), F32) for dil in DILATIONS],
        scratch_shapes=[pltpu.VMEM((tm, d), BF16), pltpu.VMEM((3 * dg // LANES, tm, LANES), F32)],
        compiler_params=_cparams("parallel", "arbitrary", "arbitrary"),
        name="dilated_qkv_prompt",
    )(x, sh, sc, g, w)


def _dilated_prompt_group(view, slopes, group):
    dg = D_GROUP_B
    dil = DILATIONS[group]
    b, tr, _ = view.shape
    t = tr * dil
    n_keys = WINDOWS[group] // dil + 1
    tq = n_keys - 1
    assert tr % tq == 0 and tq % LANES == 0

    def sec(section, prev):
        def index(bi, r, ti):
            row = jnp.maximum(ti - 1, 0) if prev else ti
            return (bi, row, r * 3 + section)
        return pl.BlockSpec((1, tq, dg), index)

    out_spec = pl.BlockSpec((1, tq, dg), lambda bi, r, ti: (bi, ti, r))
    o, lse = pl.pallas_call(
        functools.partial(_dilated_prompt_body, group=group, dilation=dil, n_keys=n_keys),
        grid=(b, dil, tr // tq),
        in_specs=[pl.BlockSpec(memory_space=pltpu.SMEM),
                  sec(0, False), sec(1, True), sec(1, False), sec(2, True), sec(2, False)],
        out_specs=[out_spec, out_spec],
        out_shape=[jax.ShapeDtypeStruct((b, tr, dil * dg), F32)] * 2,
        scratch_shapes=[pltpu.VMEM((2, HEADS_PER_GROUP_B, tq, 2 * tq), F32)],
        compiler_params=_cparams("arbitrary", "arbitrary", "arbitrary"),
        name=f"dilated_prompt_g{group}",
    )(slopes, view, view, view, view, view)
    return o.reshape(b, t, dg), lse.reshape(b, t, dg)


def _dilated_sample_body(slopes_ref, x_ref, *refs, t_dec):
    hd = HEAD_DIM_B
    dg = D_GROUP_B
    scale = hd ** -0.5
    x = x_ref[0]
    rows = x.shape[0]
    tq = lax.broadcasted_iota(jnp.int32, (rows, LANES), 0)
    mcol = lax.broadcasted_iota(jnp.int32, (rows, LANES), 1)
    tq1 = lax.broadcasted_iota(jnp.int32, (rows, 1), 0)
    o_ref = refs[-1]
    bufs, pos = [], 0
    for g in range(N_GROUPS_B):
        n_res = min(DILATIONS[g], t_dec)
        bufs.append(refs[pos:pos + n_res])
        pos += n_res
    for h in range(HEADS_PER_GROUP_B):
        outs, lses = [], []
        for g in range(N_GROUPS_B):
            dil = DILATIONS[g]
            n_back = WINDOWS[g] // dil
            slope = slopes_ref[g * HEADS_PER_GROUP_B + h]
            c0 = g * dg + h * hd
            qh = x[:, c0:c0 + hd]
            kn = x[:, N_GROUPS_B * dg + c0:N_GROUPS_B * dg + c0 + hd]
            vn = x[:, 2 * N_GROUPS_B * dg + c0:2 * N_GROUPS_B * dg + c0 + hd]
            qb = qh.astype(BF16)
            s_list, v_list = [], []
            for r in range(min(dil, t_dec)):
                kr = bufs[g][r][0, :, h * hd:(h + 1) * hd]
                v_list.append(bufs[g][r][0, :, dg + h * hd:dg + (h + 1) * hd])
                num = dil * (n_back - mcol) + (tq - r)
                ok = (num >= 0) & (num <= dil * n_back) & ((num & (dil - 1)) == 0)
                s = lax.dot_general(qb, kr.astype(BF16), _NT, preferred_element_type=F32) * scale
                s_list.append(jnp.where(ok, s - slope * num.astype(F32), NEG_INF))
            s_new = []
            for tk in range(t_dec):
                num = tq1 - tk
                ok = (num >= 0) & (num <= dil * n_back) & ((num & (dil - 1)) == 0)
                s = jnp.sum(qh * kn[tk:tk + 1, :], axis=1, keepdims=True) * scale
                s_new.append(jnp.where(ok, s - slope * num.astype(F32), NEG_INF))
            m = s_new[0]
            for s in s_new[1:]:
                m = jnp.maximum(m, s)
            for s in s_list:
                m = jnp.maximum(m, jnp.max(s, axis=1, keepdims=True))
            l = jnp.zeros((rows, 1), F32)
            o = jnp.zeros((rows, hd), F32)
            for s, v in zip(s_list, v_list):
                p = jnp.exp(s - m)
                l = l + jnp.sum(p, axis=1, keepdims=True)
                o = o + jnp.dot(p.astype(BF16), v.astype(BF16), preferred_element_type=F32)
            for tk, s in enumerate(s_new):
                p = jnp.exp(s - m)
                l = l + p
                o = o + p * vn[tk:tk + 1, :]
            outs.append(o / l)
            lses.append(m + jnp.log(l))
        lm = jnp.maximum(jnp.maximum(lses[0], lses[1]), lses[2])
        es = [jnp.exp(ls - lm) for ls in lses]
        den = es[0] + es[1] + es[2]
        merged = (es[0] * outs[0] + es[1] * outs[1] + es[2] * outs[2]) / den
        o_ref[0, :, h * hd:(h + 1) * hd] = merged[:t_dec, :]


def _dilated_sample_attn(qkv_s, states, layer, slopes):
    bs, t_dec, n_all = qkv_s.shape
    dg = D_GROUP_B
    x8 = jnp.pad(qkv_s, ((0, 0), (0, SUBLANES - t_dec), (0, 0)))
    views, specs = [], []
    for g in range(N_GROUPS_B):
        dil = DILATIONS[g]
        n_l, _, wb = states[g].shape[:3]
        n_back = WINDOWS[g] // dil
        assert wb == WINDOWS[g] == n_back * dil and n_back == LANES
        assert dil & (dil - 1) == 0 and (dil == 1 or dil >= t_dec)
        for r in range(min(dil, t_dec)):
            views.append(states[g][layer, :, r::dil].reshape(bs, n_back, 2 * dg))
            specs.append(pl.BlockSpec((1, n_back, 2 * dg), lambda bi: (bi, 0, 0)))
    return pl.pallas_call(
        functools.partial(_dilated_sample_body, t_dec=t_dec),
        grid=(bs,),
        in_specs=[pl.BlockSpec(memory_space=pltpu.SMEM),
                  pl.BlockSpec((1, SUBLANES, n_all), lambda bi: (bi, 0, 0))] + specs,
        out_specs=pl.BlockSpec((1, t_dec, dg), lambda bi: (bi, 0, 0)),
        out_shape=jax.ShapeDtypeStruct((bs, t_dec, dg), F32),
        compiler_params=_cparams("parallel"),
        name="dilated_sample_attn",
    )(slopes, x8, *views)


def _win_update_body(cur_ref, nxt_ref, new_ref, o_ref):
    wc = cur_ref.shape[1]
    t_dec = nxt_ref.shape[1]
    o_ref[0, 0:wc - t_dec] = cur_ref[0, t_dec:wc]
    last = pl.program_id(1) == pl.num_programs(1) - 1

    @pl.when(last)
    def _():
        o_ref[0, wc - t_dec:wc] = new_ref[0]

    @pl.when(jnp.logical_not(last))
    def _():
        o_ref[0, wc - t_dec:wc] = nxt_ref[0]


def _win_update(state, new):
    n_l, bs, wb = state.shape[:3]
    t_dec = new.shape[2]
    tail = state.shape[3:]
    wc = min(wb, 512)
    assert wb % wc == 0 and wc % t_dec == 0 and wc > t_dec
    n_chunks = wb // wc
    per_chunk = wc // t_dec
    zeros = (0,) * len(tail)
    flat = state.reshape((n_l * bs, wb) + tail)
    out = pl.pallas_call(
        _win_update_body,
        grid=(n_l * bs, n_chunks),
        in_specs=[pl.BlockSpec((1, wc) + tail, lambda r, c: (r, c) + zeros),
                  pl.BlockSpec((1, t_dec) + tail,
                               lambda r, c: (r, jnp.minimum(c + 1, n_chunks - 1) * per_chunk) + zeros),
                  pl.BlockSpec((1, t_dec) + tail, lambda r, c: (r, 0) + zeros)],
        out_specs=pl.BlockSpec((1, wc) + tail, lambda r, c: (r, c) + zeros),
        out_shape=jax.ShapeDtypeStruct(flat.shape, state.dtype),
        compiler_params=_cparams("parallel", "arbitrary"),
        name="win_state_update",
    )(flat, flat, new.reshape((n_l * bs, t_dec) + tail))
    return out.reshape(state.shape)


def _oproj_body(o_ref, w_ref, x_ref, g_ref, out_ref):
    y = jnp.dot(o_ref[0].astype(BF16), w_ref[...], preferred_element_type=F32)
    out_ref[0] = x_ref[0] + g_ref[0] * y


def _oproj_merge_body(o0_ref, o1_ref, o2_ref, l0_ref, l1_ref, l2_ref, w_ref, x_ref, g_ref, out_ref):
    l0, l1, l2 = l0_ref[0], l1_ref[0], l2_ref[0]
    lm = jnp.maximum(jnp.maximum(l0, l1), l2)
    e0, e1, e2 = jnp.exp(l0 - lm), jnp.exp(l1 - lm), jnp.exp(l2 - lm)
    o = (e0 * o0_ref[0] + e1 * o1_ref[0] + e2 * o2_ref[0]) / (e0 + e1 + e2)
    y = jnp.dot(o.astype(BF16), w_ref[...], preferred_element_type=F32)
    out_ref[0] = x_ref[0] + g_ref[0] * y


def _oproj(o_list, w, x, gate):
    b, t, d = x.shape
    din = w.shape[0]
    tm = _row_tile(t)
    tmod = gate.shape[1]
    tmb = tm if tmod == t else 1
    gate_map = (lambda bi, mi: (bi, mi, 0)) if tmod == t else (lambda bi, mi: (bi, 0, 0))
    row = lambda bi, mi: (bi, mi, 0)
    body = _oproj_body if len(o_list) == 1 else _oproj_merge_body
    return pl.pallas_call(
        body,
        grid=(b, t // tm),
        in_specs=[pl.BlockSpec((1, tm, din), row)] * len(o_list) + [
            pl.BlockSpec((din, d), lambda bi, mi: (0, 0)),
            pl.BlockSpec((1, tm, d), row),
            pl.BlockSpec((1, tmb, d), gate_map)],
        out_specs=pl.BlockSpec((1, tm, d), row),
        out_shape=jax.ShapeDtypeStruct((b, t, d), F32),
        compiler_params=_cparams("parallel", "parallel"),
        name="out_proj_residual",
    )(*o_list, w, x, gate)


def _conv_ffn_body(x_ref, sh_ref, sc_ref, gt_ref, g_ref, win_ref, cw_ref, cb_ref, wd_ref, p1_ref, p2_ref,
                   out_ref, a_ref, z_s, carry_s, *, chunk, seq_len, carry_rows):
    tm = x_ref.shape[1]
    dff = wd_ref.shape[0]
    x = x_ref[0]
    h = _norm_mod(x, g_ref[...], sh_ref[0], sc_ref[0]).astype(BF16)
    row = lax.broadcasted_iota(jnp.int32, (tm, chunk), 0)
    t_in_seq = row % seq_len
    if carry_rows:
        @pl.when(pl.program_id(1) == 0)
        def _():
            carry_s[...] = jnp.zeros(carry_s.shape, F32)

    for c in range(dff // chunk):
        cs = slice(c * chunk, (c + 1) * chunk)
        a = jnp.dot(h, win_ref[:, cs], preferred_element_type=F32)
        bgate = jnp.dot(h, win_ref[:, dff + c * chunk:dff + (c + 1) * chunk], preferred_element_type=F32)
        a1 = jnp.where(t_in_seq >= 1, pltpu.roll(a, 1, axis=0), 0.0)
        a2 = jnp.where(t_in_seq >= 2, pltpu.roll(a, 2, axis=0), 0.0)
        if carry_rows:
            prev = carry_s[:, cs]
            a1 = jnp.where(row == 0, prev[SUBLANES - 1:SUBLANES, :], a1)
            a2 = jnp.where(row == 0, prev[SUBLANES - 2:SUBLANES - 1, :],
                           jnp.where(row == 1, prev[SUBLANES - 1:SUBLANES, :], a2))
            carry_s[:, cs] = a[tm - SUBLANES:, :]
            a_ref[0, :, cs] = a[tm - SUBLANES:, :]
        else:
            a1 = a1 + p1_ref[0, :, cs]
            a2 = a2 + p2_ref[0, :, cs]
            a_ref[0, :, cs] = a
        y = cb_ref[:, cs] + cw_ref[0:1, cs] * a2 + cw_ref[1:2, cs] * a1 + cw_ref[2:3, cs] * a
        z_s[:, cs] = (y * jax.nn.sigmoid(y) * bgate).astype(BF16)
    f = jnp.dot(z_s[...], wd_ref[...], preferred_element_type=F32)
    out_ref[0] = x + gt_ref[0] * f


def _conv_ffn(x, sh, sc, gate, g, w_in, conv_w, conv_b, w_down, seq_len, p1=None, p2=None):
    b, t, d = x.shape
    dff = w_down.shape[0]
    tm = _row_tile(t)
    carry_rows = p1 is None
    chunk = 256
    assert dff % chunk == 0 and tm % SUBLANES == 0
    tmod = sh.shape[1]
    tmb = tm if tmod == t else 1
    mod_map = (lambda bi, mi: (bi, mi, 0)) if tmod == t else (lambda bi, mi: (bi, 0, 0))
    row = lambda bi, mi: (bi, mi, 0)
    const = lambda bi, mi: (0, 0)
    if carry_rows:
        assert seq_len == t
        p1 = p2 = jnp.zeros((1, SUBLANES, LANES), F32)
        p_spec = pl.BlockSpec((1, SUBLANES, LANES), lambda bi, mi: (0, 0, 0))
        a_rows = SUBLANES
        a_spec = pl.BlockSpec((1, SUBLANES, dff), lambda bi, mi: (bi, 0, 0))
    else:
        assert tm % seq_len == 0
        p_spec = pl.BlockSpec((1, tm, dff), row)
        a_rows = t
        a_spec = pl.BlockSpec((1, tm, dff), row)
    return pl.pallas_call(
        functools.partial(_conv_ffn_body, chunk=chunk, seq_len=seq_len, carry_rows=carry_rows),
        grid=(b, t // tm),
        in_specs=[pl.BlockSpec((1, tm, d), row),
                  pl.BlockSpec((1, tmb, d), mod_map),
                  pl.BlockSpec((1, tmb, d), mod_map),
                  pl.BlockSpec((1, tmb, d), mod_map),
                  pl.BlockSpec((1, d), const),
                  pl.BlockSpec((d, 2 * dff), const),
                  pl.BlockSpec((CONV_W, dff), const),
                  pl.BlockSpec((1, dff), const),
                  pl.BlockSpec((dff, d), const),
                  p_spec, p_spec],
        out_specs=[pl.BlockSpec((1, tm, d), row), a_spec],
        out_shape=[jax.ShapeDtypeStruct((b, t, d), F32),
                   jax.ShapeDtypeStruct((b, a_rows, dff), F32)],
        scratch_shapes=[pltpu.VMEM((tm, dff), BF16),
                        pltpu.VMEM((SUBLANES, dff), F32)],
        compiler_params=_cparams("parallel", "arbitrary"),
        name="conv_ffn",
    )(x, sh, sc, gate, g, w_in, conv_w, conv_b, w_down, p1, p2)


def _final_norm_body(x_ref, g_ref, o_ref):
    x = x_ref[0]
    ms = jnp.mean(x * x, axis=-1, keepdims=True)
    o_ref[0] = x * lax.rsqrt(ms + RMS_EPS) * g_ref[...]


def _final_norm(x, g):
    b, t, d = x.shape
    tm = _row_tile(t)
    return pl.pallas_call(
        _final_norm_body,
        grid=(b, t // tm),
        in_specs=[pl.BlockSpec((1, tm, d), lambda bi, mi: (bi, mi, 0)),
                  pl.BlockSpec((1, d), lambda bi, mi: (0, 0))],
        out_specs=pl.BlockSpec((1, tm, d), lambda bi, mi: (bi, mi, 0)),
        out_shape=jax.ShapeDtypeStruct((b, t, d), F32),
        compiler_params=_cparams("parallel", "parallel"),
        name="final_norm",
    )(x, g)


def _alibi_slopes(n_heads):
    return jnp.exp2(-8.0 * jnp.arange(1, n_heads + 1, dtype=F32) / n_heads)


def kernel(x_prompt, x_sample, cache_moba_k, cache_moba_v, page_table, state_win1, state_win2, state_win3,
           state_conv, c_prompt, c_sample, ada_w, ada_b, norm1_g, norm2_g, final_g, a_w_qkv, a_w_o,
           b_w_qkv, b_w_o, ffn_w_in, ffn_conv_w, ffn_conv_b, ffn_w_down):
    depth = ada_w.shape[0]
    bp, tp, d = x_prompt.shape
    bs, ts, _ = x_sample.shape
    n_heads_a = d // HEAD_DIM_A
    dff = ffn_w_down.shape[1]
    states = (state_win1, state_win2, state_win3)

    slopes_a = _alibi_slopes(n_heads_a)
    slopes_b = _alibi_slopes(N_GROUPS_B * HEADS_PER_GROUP_B)
    row = jnp.arange(ts * n_heads_a)
    slope_rows = jnp.broadcast_to(slopes_a[row % n_heads_a][:, None], (ts * n_heads_a, MOBA_BLOCK)).astype(F32)
    tq_rows = jnp.broadcast_to((row // n_heads_a)[:, None], (ts * n_heads_a, MOBA_BLOCK)).astype(F32)

    mods = _ada_all(jnp.concatenate([c_prompt, c_sample], axis=0), ada_w, ada_b)
    mods = mods.reshape(depth, bp + bs, 6, d)

    cache_kt = jnp.transpose(cache_moba_k, (0, 1, 3, 4, 2))
    cache_vt = jnp.transpose(cache_moba_v, (0, 1, 3, 4, 2))

    xp = x_prompt
    xs = x_sample.reshape(1, bs * ts, d)
    ka_p, va_p, ka_s, va_s = [], [], [], []
    win_p = [[] for _ in range(N_GROUPS_B)]
    win_new = [[] for _ in range(N_GROUPS_B)]
    conv_p, conv_s = [], []
    for i in range(depth):
        mp = [mods[i, :bp, k].reshape(bp, 1, d) for k in range(6)]
        ms = [jnp.repeat(mods[i, bp:, k], ts, axis=0).reshape(1, bs * ts, d) for k in range(6)]
        g1 = norm1_g[i].reshape(1, d)
        g2 = norm2_g[i].reshape(1, d)
        j = i // 2
        if i % 2 == 0:
            w_o = a_w_o[j].astype(BF16)
            q_p, kb_p, kt_p, vt_p, vtb_p, km_p = _moba_qkv_prompt(xp, mp[0], mp[1], g1, a_w_qkv[j])
            qkv_s = _nm_matmul(xs, ms[0], ms[1], g1, a_w_qkv[j].astype(BF16)).reshape(bs, ts, 3 * d)
            o_p = _moba_prompt_attn(q_p, kb_p, vtb_p, km_p, slopes_a)
            o_s = _moba_sample_attn(qkv_s, cache_kt, cache_vt, j, page_table, slope_rows, tq_rows)
            xp = _oproj([o_p], w_o, xp, mp[2])
            xs = _oproj([o_s.reshape(1, bs * ts, d)], w_o, xs, ms[2])
            ka_p.append(kt_p.reshape(bp, n_heads_a, HEAD_DIM_A, tp))
            va_p.append(vt_p.reshape(bp, n_heads_a, HEAD_DIM_A, tp))
            ka_s.append(qkv_s[:, :, d:2 * d].reshape(bs, ts, n_heads_a, HEAD_DIM_A))
            va_s.append(qkv_s[:, :, 2 * d:].reshape(bs, ts, n_heads_a, HEAD_DIM_A))
        else:
            w_o = b_w_o[j].astype(BF16)
            views_p = _dilated_qkv_prompt(xp, mp[0], mp[1], g1, b_w_qkv[j])
            qkv_s = _nm_matmul(xs, ms[0], ms[1], g1, b_w_qkv[j].astype(BF16)).reshape(bs, ts, -1)
            parts = [_dilated_prompt_group(views_p[g], slopes_b, g) for g in range(N_GROUPS_B)]
            xp = _oproj([p[0] for p in parts] + [p[1] for p in parts], w_o, xp, mp[2])
            o_s = _dilated_sample_attn(qkv_s, states, j, slopes_b)
            xs = _oproj([o_s.reshape(1, bs * ts, D_GROUP_B)], w_o, xs, ms[2])
            kv_s = qkv_s.reshape(bs, ts, 3, N_GROUPS_B, HEADS_PER_GROUP_B, HEAD_DIM_B)[:, :, 1:]
            for g in range(N_GROUPS_B):
                wp = min(WINDOWS[g], tp)
                dil = DILATIONS[g]
                assert wp % dil == 0
                tail = views_p[g][:, (tp - wp) // dil:].reshape(
                    bp, wp, 3, HEADS_PER_GROUP_B, HEAD_DIM_B)
                win_p[g].append(tail[:, :, 1:])
                win_new[g].append(kv_s[:, :, :, g])
        w_in = ffn_w_in[i].astype(BF16)
        w_down = ffn_w_down[i].astype(BF16)
        cw = ffn_conv_w[i]
        cb = ffn_conv_b[i].reshape(1, dff)
        xp, a_tail = _conv_ffn(xp, mp[3], mp[4], mp[5], g2, w_in, cw, cb, w_down, tp)
        conv_p.append(a_tail[:, -(CONV_W - 1):])
        st = state_conv[i]
        zero = jnp.zeros((bs, 1, dff), F32)
        p1 = jnp.concatenate([st[:, 1:2], jnp.tile(zero, (1, ts - 1, 1))], axis=1).reshape(1, bs * ts, dff)
        p2 = jnp.concatenate([st[:, 0:1], st[:, 1:2], jnp.tile(zero, (1, ts - 2, 1))], axis=1).reshape(1, bs * ts, dff)
        xs, a_s = _conv_ffn(xs, ms[3], ms[4], ms[5], g2, w_in, cw, cb, w_down, ts, p1, p2)
        a_ext = jnp.concatenate([st, a_s.reshape(bs, ts, dff)], axis=1)
        conv_s.append(a_ext[:, -(CONV_W - 1):])
    y_prompt = _final_norm(xp, final_g.reshape(1, d))
    y_sample = _final_norm(xs, final_g.reshape(1, d)).reshape(bs, ts, d)
    win_s = [_win_update(states[g], jnp.stack(win_new[g])) for g in range(N_GROUPS_B)]
    moba_k_prompt = jnp.transpose(jnp.stack(ka_p), (0, 1, 4, 2, 3))
    moba_v_prompt = jnp.transpose(jnp.stack(va_p), (0, 1, 4, 2, 3))
    return (y_prompt, y_sample, moba_k_prompt, moba_v_prompt, jnp.stack(ka_s), jnp.stack(va_s),
            jnp.stack(win_p[0]), jnp.stack(win_p[1]), jnp.stack(win_p[2]),
            win_s[0], win_s[1], win_s[2],
            jnp.stack(conv_p), jnp.stack(conv_s))
```
